```python
import jax, jax.numpy as jnp
from jax import lax
import numpy as np

D_MODEL = 2048
BATCH = 2
SEQ = 4096
DEPTH = 1

GLA_HEADS = 4
GLA_DK = 128
GLA_DV = 256
GLA_RANK = 16
GLA_GATE_TEMP = 16.0
GLA_CHUNK = 64
GLA_K = GLA_HEADS * GLA_DK
GLA_V = GLA_HEADS * GLA_DV
MOBA_HEADS = 8
MOBA_HD = 128
MOBA_W = MOBA_HEADS * MOBA_HD
MOBA_BLOCK = 256
MOBA_TOPK = 3
MOBA_QCHUNK = 32
IN_SIZES = (GLA_K, GLA_K, GLA_V, GLA_V, GLA_RANK, MOBA_W, MOBA_W, MOBA_W, D_MODEL, D_MODEL)
IN_WIDTH = sum(IN_SIZES)
FFN_DIM = 5632
CONV_WIDTH = 3
N_MOD = 6
EPS = 1e-6

kernel_name = "hybrid_gla_moba_convffn_block"


def rmsnorm(x, gain):
    xf = x.astype(jnp.float32)
    y = xf * lax.rsqrt(jnp.mean(xf * xf, axis=-1, keepdims=True) + EPS) * gain.astype(jnp.float32)
    return y.astype(x.dtype)


def gla_chunked(q, k, v, log_a):
    B, S, H, dk = q.shape
    dv = v.shape[-1]
    C = GLA_CHUNK
    N = S // C

    def to_chunks(t):
        return t.reshape(B, N, C, H, t.shape[-1]).transpose(1, 0, 3, 2, 4)

    qs, ks, vs, gs = (to_chunks(t.astype(jnp.float32)) for t in (q * (dk ** -0.5), k, v, log_a))
    causal = jnp.tril(jnp.ones((C, C), dtype=bool))

    def step(state, inp):
        qc, kc, vc, gc = inp
        b = jnp.cumsum(gc, axis=2)
        o_inter = jnp.einsum('bhcd,bhde->bhce', qc * jnp.exp(b), state)
        diff = b[:, :, :, None, :] - b[:, :, None, :, :]
        decay = jnp.exp(jnp.where(causal[None, None, :, :, None], diff, -jnp.inf))
        attn = jnp.einsum('bhid,bhjd,bhijd->bhij', qc, kc, decay)
        o = o_inter + jnp.einsum('bhij,bhje->bhie', attn, vc)
        b_last = b[:, :, -1:, :]
        state = (jnp.exp(b_last[:, :, 0, :])[..., None] * state
                 + jnp.einsum('bhcd,bhce->bhde', kc * jnp.exp(b_last - b), vc))
        return state, o

    s0 = jnp.zeros((B, H, dk, dv), jnp.float32)
    _, outs = lax.scan(step, s0, (qs, ks, vs, gs))
    return outs.transpose(1, 0, 3, 2, 4).reshape(B, S, H, dv)


def moba_attention(q, k, v):
    B, S, H, hd = q.shape
    BLK = MOBA_BLOCK
    nb = -(-S // BLK)
    Sp = nb * BLK
    topk = min(MOBA_TOPK, nb - 1)
    scale = hd ** -0.5
    q = q.transpose(0, 2, 1, 3)
    pad = ((0, 0), (0, 0), (0, Sp - S), (0, 0))
    kblk = jnp.pad(k.transpose(0, 2, 1, 3), pad).reshape(B, H, nb, BLK, hd)
    vblk = jnp.pad(v.transpose(0, 2, 1, 3), pad).reshape(B, H, nb, BLK, hd)
    q_block = jnp.arange(S) // BLK

    if topk > 0:
        kmean = jnp.mean(kblk.astype(jnp.float32), axis=3)
        sblk = jnp.einsum('bhsd,bhnd->bhsn', q.astype(jnp.float32), kmean)
        past = jnp.arange(nb)[None, :] < q_block[:, None]
        sblk = jnp.where(past[None, None], sblk, -1e30)
        _, sel_idx = lax.top_k(sblk, topk)
        sel_valid = sel_idx < q_block[None, None, :, None]
    bi = jnp.arange(B)[:, None, None, None]
    hi = jnp.arange(H)[None, :, None, None]

    def chunk(ci):
        start = ci * MOBA_QCHUNK
        qc = lax.dynamic_slice_in_dim(q, start, MOBA_QCHUNK, axis=2)
        blk = start // BLK
        kown = lax.dynamic_index_in_dim(kblk, blk, axis=2, keepdims=False)
        vown = lax.dynamic_index_in_dim(vblk, blk, axis=2, keepdims=False)
        qpos = start + jnp.arange(MOBA_QCHUNK)
        kpos = blk * BLK + jnp.arange(BLK)
        s_own = jnp.einsum('bhqd,bhtd->bhqt', qc, kown).astype(jnp.float32) * scale
        s_own = jnp.where((kpos[None, :] <= qpos[:, None])[None, None], s_own, -jnp.inf)
        if topk == 0:
            p = jax.nn.softmax(s_own, axis=-1)
            return jnp.einsum('bhqt,bhtd->bhqd', p.astype(v.dtype), vown)
        idx = lax.dynamic_slice_in_dim(sel_idx, start, MOBA_QCHUNK, axis=2)
        valid = lax.dynamic_slice_in_dim(sel_valid, start, MOBA_QCHUNK, axis=2)
        ksel = kblk[bi, hi, idx]
        vsel = vblk[bi, hi, idx]
        s_sel = jnp.einsum('bhqd,bhqjtd->bhqjt', qc, ksel).astype(jnp.float32) * scale
        s_sel = jnp.where(valid[..., None], s_sel, -jnp.inf)
        s_all = jnp.concatenate([s_sel.reshape(B, H, MOBA_QCHUNK, topk * BLK), s_own], axis=-1)
        p = jax.nn.softmax(s_all, axis=-1).astype(v.dtype)
        p_sel = p[..., :topk * BLK].reshape(B, H, MOBA_QCHUNK, topk, BLK)
        p_own = p[..., topk * BLK:]
        return (jnp.einsum('bhqjt,bhqjtd->bhqd', p_sel, vsel)
                + jnp.einsum('bhqt,bhtd->bhqd', p_own, vown))

    outs = lax.map(chunk, jnp.arange(S // MOBA_QCHUNK))
    out = outs.transpose(1, 2, 0, 3, 4).reshape(B, H, S, hd)
    return out.transpose(0, 2, 1, 3).reshape(B, S, H * hd)


def token_mixer(h, w_in, w_alpha_up, b_alpha, g_head, w_gla_out, w_moba_out, w_o):
    B, S, _ = h.shape
    proj = h @ w_in
    offs = np.cumsum(IN_SIZES)[:-1].tolist()
    gq, gk, gv, gr, ga_low, mq, mk, mv, gate_a, gate_b = jnp.split(proj, offs, axis=-1)
    log_a = jax.nn.log_sigmoid((ga_low @ w_alpha_up + b_alpha).astype(jnp.float32)) / GLA_GATE_TEMP
    o_gla = gla_chunked(gq.reshape(B, S, GLA_HEADS, GLA_DK), gk.reshape(B, S, GLA_HEADS, GLA_DK),
                        gv.reshape(B, S, GLA_HEADS, GLA_DV), log_a.reshape(B, S, GLA_HEADS, GLA_DK))
    o_gla = rmsnorm(o_gla.astype(h.dtype), g_head) * jax.nn.silu(gr.reshape(B, S, GLA_HEADS, GLA_DV))
    y_a = o_gla.reshape(B, S, GLA_V) @ w_gla_out
    o_moba = moba_attention(mq.reshape(B, S, MOBA_HEADS, MOBA_HD), mk.reshape(B, S, MOBA_HEADS, MOBA_HD),
                            mv.reshape(B, S, MOBA_HEADS, MOBA_HD))
    y_b = o_moba @ w_moba_out
    merged = jax.nn.sigmoid(gate_a) * y_a + jax.nn.sigmoid(gate_b) * y_b
    return merged @ w_o


def conv_ffn(h, w_up, w_conv, b_conv, w_down):
    up = h @ w_up
    C = up.shape[-1]
    up = lax.conv_general_dilated(up, w_conv[:, None, :], window_strides=(1,),
                                  padding=[(CONV_WIDTH - 1, 0)],
                                  dimension_numbers=('NWC', 'WIO', 'NWC'),
                                  feature_group_count=C) + b_conv
    u, g = jnp.split(up, 2, axis=-1)
    return (jax.nn.gelu(g) * u) @ w_down


def setup_inputs(seed: int = 0) -> dict:
    key = jax.random.key(seed)
    ks = jax.random.split(key, 20)
    D, L = D_MODEL, DEPTH

    def w(k, shape, fan_in, s=1.0):
        return jax.random.normal(k, shape, jnp.float32) * (s * fan_in ** -0.5)

    def gain(k, shape):
        return 1.0 + 0.05 * jax.random.normal(k, shape, jnp.float32)

    return {
        "x": jax.random.normal(ks[0], (BATCH, SEQ, D), jnp.float32),
        "c": jax.random.normal(ks[1], (BATCH, D), jnp.float32),
        "w_ada": w(ks[2], (L, D, N_MOD * D), D, 0.5),
        "b_ada": 0.02 * jax.random.normal(ks[3], (L, N_MOD * D), jnp.float32),
        "g_pre_mix": gain(ks[4], (L, D)),
        "w_in": w(ks[5], (L, D, IN_WIDTH), D),
        "w_alpha_up": w(ks[6], (L, GLA_RANK, GLA_K), GLA_RANK),
        "b_alpha": 0.1 * jax.random.normal(ks[7], (L, GLA_K), jnp.float32),
        "g_gla_head": gain(ks[8], (L, GLA_DV)),
        "w_gla_out": w(ks[9], (L, GLA_V, D), GLA_V),
        "w_moba_out": w(ks[10], (L, MOBA_W, D), MOBA_W),
        "w_o": w(ks[11], (L, D, D), D),
        "g_post_mix": gain(ks[12], (L, D)),
        "g_pre_ffn": gain(ks[13], (L, D)),
        "w_up": w(ks[14], (L, D, 2 * FFN_DIM), D),
        "w_conv": w(ks[15], (L, CONV_WIDTH, 2 * FFN_DIM), CONV_WIDTH),
        "b_conv": 0.02 * jax.random.normal(ks[16], (L, 2 * FFN_DIM), jnp.float32),
        "w_down": w(ks[17], (L, FFN_DIM, D), FFN_DIM),
        "g_post_ffn": gain(ks[18], (L, D)),
    }


def reference(x, c, w_ada, b_ada, g_pre_mix, w_in, w_alpha_up, b_alpha, g_gla_head,
              w_gla_out, w_moba_out, w_o, g_post_mix, g_pre_ffn, w_up, w_conv, b_conv,
              w_down, g_post_ffn):
    for l in range(DEPTH):
        mod = jax.nn.silu(c) @ w_ada[l] + b_ada[l]
        sh1, sc1, gt1, sh2, sc2, gt2 = (m[:, None, :] for m in jnp.split(mod, N_MOD, axis=-1))
        h = rmsnorm(x, g_pre_mix[l]) * (1.0 + sc1) + sh1
        y = token_mixer(h, w_in[l], w_alpha_up[l], b_alpha[l], g_gla_head[l],
                        w_gla_out[l], w_moba_out[l], w_o[l])
        x = x + gt1 * rmsnorm(y, g_post_mix[l])
        h = rmsnorm(x, g_pre_ffn[l]) * (1.0 + sc2) + sh2
        y = conv_ffn(h, w_up[l], w_conv[l], b_conv[l], w_down[l])
        x = x + gt2 * rmsnorm(y, g_post_ffn[l])
    return x
```

```python
import functools

import numpy as np
import jax
import jax.numpy as jnp
from jax import lax
from jax.experimental import pallas as pl
from jax.experimental.pallas import tpu as pltpu

F32 = jnp.float32
BF16 = jnp.bfloat16

GLA_DK = 128
GLA_RANK = 16
GLA_GATE_TEMP = 16.0
GLA_CHUNK = 64
MOBA_HD = 128
MOBA_BLOCK = 256
MOBA_TOPK = 3
CONV_WIDTH = 3
N_MOD = 6
EPS = 1e-6

V7X_LANES = 128
V7X_VMEM_LIMIT_BYTES = 56 * 1024 * 1024

IN_TILE = 1024
MASK_VALUE = -1e30


def _params(*semantics):
    return pltpu.CompilerParams(dimension_semantics=semantics,
                                vmem_limit_bytes=V7X_VMEM_LIMIT_BYTES)


def _dot(a, b):
    return jnp.dot(a, b, preferred_element_type=F32)


def _dot_nt(a, b):
    return lax.dot_general(a, b, (((1,), (1,)), ((), ())), preferred_element_type=F32)


def _dot_tn(a, b):
    return lax.dot_general(a, b, (((0,), (0,)), ((), ())), preferred_element_type=F32)


def _split3(x):
    x1 = x.astype(BF16)
    r1 = x - x1.astype(F32)
    x2 = r1.astype(BF16)
    x3 = (r1 - x2.astype(F32)).astype(BF16)
    return x1, x2, x3


def _rms(x, gain):
    return x * lax.rsqrt(jnp.mean(x * x, axis=-1, keepdims=True) + EPS) * gain


def _mod_kernel(c_ref, w_ref, b_ref, o_ref):
    c = c_ref[...]
    s = c * jax.nn.sigmoid(c)
    o_ref[...] = _dot(s.astype(BF16), w_ref[...].astype(BF16)) + b_ref[...]


def _modulation(c8, w_ada, b_ada3, l):
    rows, d = c8.shape
    n = w_ada.shape[-1]
    tn = 1024
    return pl.pallas_call(
        _mod_kernel,
        grid=(n // tn,),
        in_specs=[pl.BlockSpec((rows, d), lambda j: (0, 0)),
                  pl.BlockSpec((None, d, tn), lambda j: (l, 0, j)),
                  pl.BlockSpec((None, 1, tn), lambda j: (l, 0, j))],
        out_specs=pl.BlockSpec((rows, tn), lambda j: (0, j)),
        out_shape=jax.ShapeDtypeStruct((rows, n), F32),
        compiler_params=_params("arbitrary"),
        name="adaln_mod",
    )(c8, w_ada, b_ada3)


def _prenorm_kernel(x_ref, g_ref, mod_ref, o_ref, *, shift_row, scale_row):
    y = _rms(x_ref[...], g_ref[...])
    sh = mod_ref[shift_row:shift_row + 1, :]
    sc = mod_ref[scale_row:scale_row + 1, :]
    o_ref[...] = (y * (1.0 + sc) + sh).astype(o_ref.dtype)


def _prenorm(x2, gain3, mod3, l, seq, shift_row, scale_row):
    m, d = x2.shape
    tm = 512
    per_batch = seq // tm
    return pl.pallas_call(
        functools.partial(_prenorm_kernel, shift_row=shift_row, scale_row=scale_row),
        grid=(m // tm,),
        in_specs=[pl.BlockSpec((tm, d), lambda i: (i, 0)),
                  pl.BlockSpec((None, 1, d), lambda i: (l, 0, 0)),
                  pl.BlockSpec((None, N_MOD, d), lambda i: (i // per_batch, 0, 0))],
        out_specs=pl.BlockSpec((tm, d), lambda i: (i, 0)),
        out_shape=jax.ShapeDtypeStruct((m, d), BF16),
        compiler_params=_params("arbitrary"),
        name="prenorm_mix",
    )(x2, gain3, mod3)


def _inproj_kernel(h_ref, wa_ref, wb_ref, o_ref, wbf_ref, *, first_shifted):
    j = pl.program_id(0)
    i = pl.program_id(1)
    keep = IN_TILE - GLA_RANK

    @pl.when(jnp.logical_and(i == 0, j < first_shifted))
    def _():
        wbf_ref[...] = wa_ref[...].astype(BF16)

    @pl.when(jnp.logical_and(i == 0, j >= first_shifted))
    def _():
        wbf_ref[:, :keep] = wa_ref[:, GLA_RANK:].astype(BF16)
        wbf_ref[:, keep:] = wb_ref[:, :GLA_RANK].astype(BF16)

    o_ref[...] = _dot(h_ref[...], wbf_ref[...]).astype(o_ref.dtype)


def _in_projection(h, w_in, l, n_tiles, first_shifted):
    m, d = h.shape
    tm = 1024
    lane_blocks = IN_TILE // V7X_LANES
    return pl.pallas_call(
        functools.partial(_inproj_kernel, first_shifted=first_shifted),
        grid=(n_tiles, m // tm),
        in_specs=[pl.BlockSpec((tm, d), lambda j, i: (i, 0)),
                  pl.BlockSpec((None, d, IN_TILE), lambda j, i: (l, 0, j)),
                  pl.BlockSpec((None, d, V7X_LANES), lambda j, i: (l, 0, (j + 1) * lane_blocks))],
        out_specs=pl.BlockSpec((None, tm, IN_TILE), lambda j, i: (j, i, 0)),
        out_shape=jax.ShapeDtypeStruct((n_tiles, m, IN_TILE), BF16),
        scratch_shapes=[pltpu.VMEM((d, IN_TILE), BF16)],
        compiler_params=_params("arbitrary", "arbitrary"),
        name="in_projection",
    )(h, w_in, w_in)


def _loga_kernel(h_ref, w_ref, wup_ref, b_ref, o_ref):
    low = _dot(h_ref[...], w_ref[...].astype(BF16))
    z = _dot(low.astype(BF16), wup_ref[...].astype(BF16)) + b_ref[...]
    log_sig = jnp.minimum(z, 0.0) - jnp.log1p(jnp.exp(-jnp.abs(z)))
    o_ref[...] = log_sig / GLA_GATE_TEMP


def _log_alpha(h, w_in, wup_pad, b_alpha3, l, low_block):
    m, d = h.shape
    k = wup_pad.shape[-1]
    tm = 1024
    return pl.pallas_call(
        _loga_kernel,
        grid=(m // tm,),
        in_specs=[pl.BlockSpec((tm, d), lambda i: (i, 0)),
                  pl.BlockSpec((None, d, V7X_LANES), lambda i: (l, 0, low_block)),
                  pl.BlockSpec((None, V7X_LANES, k), lambda i: (l, 0, 0)),
                  pl.BlockSpec((None, 1, k), lambda i: (l, 0, 0))],
        out_specs=pl.BlockSpec((tm, k), lambda i: (i, 0)),
        out_shape=jax.ShapeDtypeStruct((m, k), F32),
        compiler_params=_params("arbitrary"),
        name="gla_log_alpha",
    )(h, w_in, wup_pad, b_alpha3)


def _gla_tables(c):
    levels = int(np.log2(c))
    t = np.arange(c)
    tabs = [(t[None, :] <= t[:, None]),
            (t[None, :] > t[:, None])]
    for lv in range(1, levels + 1):
        size, half = 1 << lv, 1 << (lv - 1)
        mid = (t // size) * size + half
        second = t >= mid
        q_side = second[:, None] & (t[None, :] >= mid[:, None]) & (t[None, :] <= t[:, None])
        k_side = (~second)[:, None] & (t[None, :] > t[:, None]) & (t[None, :] < mid[:, None])
        tabs.append(q_side | k_side)
    e = np.concatenate(tabs, axis=0).astype(np.float32)
    x = t[:, None] ^ t[None, :]
    lvl = np.where(x == 0, 0, np.floor(np.log2(np.maximum(x, 1))).astype(np.int64) + 1)
    lvl = np.where(t[:, None] >= t[None, :], lvl, -1).astype(np.int32)
    return e, lvl, levels


def _gla_kernel(q_ref, k_ref, v_ref, r_ref, la_ref, gh_ref, e_ref, lvl_ref, o_ref, st_ref,
                *, heads, dv, levels, n_chunks):
    c = GLA_CHUNK
    dk = GLA_DK

    @pl.when(pl.program_id(1) == 0)
    def _():
        st_ref[...] = jnp.zeros_like(st_ref)

    e_tab = e_ref[...]
    lvl = lvl_ref[...]
    row = lax.broadcasted_iota(jnp.int32, (c, 1), 0)
    gain = gh_ref[...]

    def chunk(ci, carry):
        rows = pl.ds(pl.multiple_of(ci * c, c), c)
        g1, g2, g3 = _split3(la_ref[rows, :])
        ex = jnp.exp(_dot(e_tab, g1) + _dot(e_tab, g2) + _dot(e_tab, g3))
        ex_b = ex[0:c]
        ex_last = ex[c:2 * c]
        q_all = q_ref[rows, :].astype(F32) * (dk ** -0.5)
        k_all = k_ref[rows, :].astype(F32)
        for h in range(heads):
            sl = slice(h * dk, (h + 1) * dk)
            vs = slice(h * dv, (h + 1) * dv)
            qh, kh = q_all[:, sl], k_all[:, sl]
            vh = v_ref[rows, vs]
            st = st_ref[h]
            o = _dot_nt((qh * ex_b[:, sl]).astype(BF16), st.astype(BF16))
            a = jnp.where(lvl == 0, _dot_nt(qh.astype(BF16), kh.astype(BF16)), 0.0)
            for lv in range(1, levels + 1):
                e_lv = ex[(1 + lv) * c:(2 + lv) * c, sl]
                second = ((row >> (lv - 1)) & 1) == 1
                ql = jnp.where(second, qh * e_lv, 0.0).astype(BF16)
                kl = jnp.where(second, 0.0, kh * e_lv).astype(BF16)
                a = a + jnp.where(lvl == lv, _dot_nt(ql, kl), 0.0)
            o = o + _dot(a.astype(BF16), vh)
            kt = (kh * ex_last[:, sl]).astype(BF16)
            st_ref[h] = st * ex_b[c - 1:c, sl] + _dot_tn(vh, kt)
            y = _rms(o, gain)
            r = r_ref[rows, vs].astype(F32)
            o_ref[rows, vs] = (y * (r * jax.nn.sigmoid(r))).astype(o_ref.dtype)
        return carry

    lax.fori_loop(0, n_chunks, chunk, 0)


def _gla(proj, log_a, g_head3, l, batch, seq, heads, dv):
    n_t, m, _ = proj.shape
    dk = GLA_DK
    rows = 512
    per_batch = seq // rows
    e_np, lvl_np, levels = _gla_tables(GLA_CHUNK)
    e_tab = jnp.asarray(e_np, BF16)
    lvl = jnp.asarray(lvl_np)
    kw = heads * dk
    assert 2 * kw == IN_TILE and heads * dv == IN_TILE
    kern = functools.partial(_gla_kernel, heads=heads, dv=dv, levels=levels,
                             n_chunks=rows // GLA_CHUNK)
    return pl.pallas_call(
        kern,
        grid=(batch, per_batch),
        in_specs=[pl.BlockSpec((None, rows, kw), lambda b, r: (0, b * per_batch + r, 0)),
                  pl.BlockSpec((None, rows, kw), lambda b, r: (0, b * per_batch + r, 1)),
                  pl.BlockSpec((None, rows, IN_TILE), lambda b, r: (1, b * per_batch + r, 0)),
                  pl.BlockSpec((None, rows, IN_TILE), lambda b, r: (2, b * per_batch + r, 0)),
                  pl.BlockSpec((rows, kw), lambda b, r: (b * per_batch + r, 0)),
                  pl.BlockSpec((None, 1, dv), lambda b, r: (l, 0, 0)),
                  pl.BlockSpec(e_np.shape, lambda b, r: (0, 0)),
                  pl.BlockSpec(lvl_np.shape, lambda b, r: (0, 0))],
        out_specs=pl.BlockSpec((rows, IN_TILE), lambda b, r: (b * per_batch + r, 0)),
        out_shape=jax.ShapeDtypeStruct((m, IN_TILE), BF16),
        scratch_shapes=[pltpu.VMEM((heads, dv, dk), F32)],
        compiler_params=_params("arbitrary", "arbitrary"),
        name="gla",
    )(proj, proj, proj, proj, log_a, g_head3, e_tab, lvl)


def _moba_kernel(q_ref, k_ref, v_ref, o_ref, kmean_ref, vt_ref, bias_ref, *, n_blocks, topk):
    blk = MOBA_BLOCK
    i = pl.program_id(2)
    scale = MOBA_HD ** -0.5

    @pl.when(i == 0)
    def _():
        for n in range(n_blocks):
            kb = k_ref[n * blk:(n + 1) * blk, :].astype(F32)
            kmean_ref[n:n + 1, :] = jnp.mean(kb, axis=0, keepdims=True)
            vt_ref[:, n * blk:(n + 1) * blk] = v_ref[n * blk:(n + 1) * blk, :].astype(F32).T.astype(BF16)

    q = q_ref[...]

    m1, m2, m3 = _split3(kmean_ref[...])
    s_blk = _dot_nt(m1, q) + _dot_nt(m2, q) + _dot_nt(m3, q)
    bidx = lax.broadcasted_iota(jnp.int32, s_blk.shape, 0)
    cur = jnp.where(bidx < i, s_blk, MASK_VALUE)
    sel = jnp.zeros(s_blk.shape, jnp.bool_)
    for _ in range(topk):
        mx = jnp.max(cur, axis=0, keepdims=True)
        idx = jnp.min(jnp.where(cur == mx, bidx, n_blocks), axis=0, keepdims=True)
        pick = bidx == idx
        sel = jnp.logical_or(sel, jnp.logical_and(pick, idx < i))
        cur = jnp.where(pick, -jnp.inf, cur)
    bias_ref[...] = jnp.where(sel, 0.0, MASK_VALUE)

    own = pl.ds(pl.multiple_of(i * blk, blk), blk)
    s = _dot_nt(k_ref[own, :], q) * scale
    kpos = lax.broadcasted_iota(jnp.int32, s.shape, 0)
    qpos = lax.broadcasted_iota(jnp.int32, s.shape, 1)
    s = jnp.where(kpos <= qpos, s, MASK_VALUE)
    m = jnp.max(s, axis=0, keepdims=True)
    p = jnp.exp(s - m)
    l = jnp.sum(p, axis=0, keepdims=True)
    acc = _dot(vt_ref[:, own], p.astype(BF16))

    def past(j, carry):
        m, l, acc = carry
        cols = pl.ds(pl.multiple_of(j * blk, blk), blk)
        s = _dot_nt(k_ref[cols, :], q) * scale + bias_ref[pl.ds(j, 1), :]
        m_new = jnp.maximum(m, jnp.max(s, axis=0, keepdims=True))
        alpha = jnp.exp(m - m_new)
        p = jnp.exp(s - m_new)
        l = l * alpha + jnp.sum(p, axis=0, keepdims=True)
        acc = acc * alpha + _dot(vt_ref[:, cols], p.astype(BF16))
        return m_new, l, acc

    m, l, acc = lax.fori_loop(0, i, past, (m, l, acc))
    o_ref[...] = (acc / l).T.astype(o_ref.dtype)


def _moba(proj, batch, seq, heads):
    n_t, m, _ = proj.shape
    hd, blk = MOBA_HD, MOBA_BLOCK
    assert seq % blk == 0 and heads * hd == IN_TILE
    n_blocks = seq // blk
    topk = min(MOBA_TOPK, n_blocks - 1)
    kern = functools.partial(_moba_kernel, n_blocks=n_blocks, topk=topk)
    return pl.pallas_call(
        kern,
        grid=(batch, heads, n_blocks),
        in_specs=[pl.BlockSpec((None, blk, hd), lambda b, h, i: (3, b * n_blocks + i, h)),
                  pl.BlockSpec((None, seq, hd), lambda b, h, i: (4, b, h)),
                  pl.BlockSpec((None, seq, hd), lambda b, h, i: (5, b, h))],
        out_specs=pl.BlockSpec((blk, hd), lambda b, h, i: (b * n_blocks + i, h)),
        out_shape=jax.ShapeDtypeStruct((m, IN_TILE), BF16),
        scratch_shapes=[pltpu.VMEM((n_blocks, hd), F32),
                        pltpu.VMEM((hd, seq), BF16),
                        pltpu.VMEM((n_blocks, blk), F32)],
        compiler_params=_params("arbitrary", "arbitrary", "arbitrary"),
        name="moba",
    )(proj, proj, proj)


def _merge_kernel(og_ref, om_ref, wg_ref, wm_ref, ga_ref, gb_ref, o_ref, wgb_ref, wmb_ref):
    @pl.when(pl.program_id(1) == 0)
    def _():
        wgb_ref[...] = wg_ref[...].astype(BF16)
        wmb_ref[...] = wm_ref[...].astype(BF16)

    ya = _dot(og_ref[...], wgb_ref[...])
    yb = _dot(om_ref[...], wmb_ref[...])
    ga = jax.nn.sigmoid(ga_ref[...].astype(F32))
    gb = jax.nn.sigmoid(gb_ref[...].astype(F32))
    o_ref[...] = (ga * ya + gb * yb).astype(o_ref.dtype)


def _merge(o_gla, o_moba, w_gla_out, w_moba_out, proj, l, d):
    m, kg = o_gla.shape
    km = o_moba.shape[1]
    tm, tn = 1024, IN_TILE
    n_tiles = d // tn
    gate_a0 = 6
    gate_b0 = 6 + n_tiles
    return pl.pallas_call(
        _merge_kernel,
        grid=(n_tiles, m // tm),
        in_specs=[pl.BlockSpec((tm, kg), lambda j, i: (i, 0)),
                  pl.BlockSpec((tm, km), lambda j, i: (i, 0)),
                  pl.BlockSpec((None, kg, tn), lambda j, i: (l, 0, j)),
                  pl.BlockSpec((None, km, tn), lambda j, i: (l, 0, j)),
                  pl.BlockSpec((None, tm, tn), lambda j, i: (gate_a0 + j, i, 0)),
                  pl.BlockSpec((None, tm, tn), lambda j, i: (gate_b0 + j, i, 0))],
        out_specs=pl.BlockSpec((tm, tn), lambda j, i: (i, j)),
        out_shape=jax.ShapeDtypeStruct((m, d), BF16),
        scratch_shapes=[pltpu.VMEM((kg, tn), BF16), pltpu.VMEM((km, tn), BF16)],
        compiler_params=_params("arbitrary", "arbitrary"),
        name="branch_merge",
    )(o_gla, o_moba, w_gla_out, w_moba_out, proj, proj)


def _matmul_kernel(a_ref, w_ref, o_ref, wbf_ref):
    @pl.when(pl.program_id(1) == 0)
    def _():
        wbf_ref[...] = w_ref[...].astype(BF16)

    o_ref[...] = _dot(a_ref[...], wbf_ref[...]).astype(o_ref.dtype)


def _matmul(a, w, l, tm, tn, name):
    m, k = a.shape
    n = w.shape[-1]
    return pl.pallas_call(
        _matmul_kernel,
        grid=(n // tn, m // tm),
        in_specs=[pl.BlockSpec((tm, k), lambda j, i: (i, 0)),
                  pl.BlockSpec((None, k, tn), lambda j, i: (l, 0, j))],
        out_specs=pl.BlockSpec((tm, tn), lambda j, i: (i, j)),
        out_shape=jax.ShapeDtypeStruct((m, n), F32),
        scratch_shapes=[pltpu.VMEM((k, tn), BF16)],
        compiler_params=_params("arbitrary", "arbitrary"),
        name=name,
    )(a, w)


def _post_kernel(x_ref, y_ref, gpost_ref, mod_ref, *rest, gate_row, next_rows):
    x = x_ref[...] + mod_ref[gate_row:gate_row + 1, :] * _rms(y_ref[...], gpost_ref[...])
    if next_rows is None:
        (o_ref,) = rest
        o_ref[...] = x
    else:
        gpre_ref, o_ref, h_ref = rest
        shift_row, scale_row = next_rows
        o_ref[...] = x
        h = _rms(x, gpre_ref[...])
        h_ref[...] = (h * (1.0 + mod_ref[scale_row:scale_row + 1, :])
                      + mod_ref[shift_row:shift_row + 1, :]).astype(h_ref.dtype)


def _post(x2, y, gpost3, mod3, l, seq, gate_row, gpre3=None, next_rows=None):
    m, d = x2.shape
    tm = 512
    per_batch = seq // tm
    row_spec = pl.BlockSpec((tm, d), lambda i: (i, 0))
    gain_spec = pl.BlockSpec((None, 1, d), lambda i: (l, 0, 0))
    in_specs = [row_spec, row_spec, gain_spec,
                pl.BlockSpec((None, N_MOD, d), lambda i: (i // per_batch, 0, 0))]
    args = [x2, y, gpost3, mod3]
    if next_rows is None:
        out_specs, out_shape = row_spec, jax.ShapeDtypeStruct((m, d), F32)
    else:
        in_specs.append(gain_spec)
        args.append(gpre3)
        out_specs = (row_spec, row_spec)
        out_shape = (jax.ShapeDtypeStruct((m, d), F32), jax.ShapeDtypeStruct((m, d), BF16))
    return pl.pallas_call(
        functools.partial(_post_kernel, gate_row=gate_row, next_rows=next_rows),
        grid=(m // tm,),
        in_specs=in_specs,
        out_specs=out_specs,
        out_shape=out_shape,
        compiler_params=_params("arbitrary"),
        name="post_norm_residual",
    )(*args)


def _ffn_up_kernel(h_ref, wu_ref, wg_ref, cu_ref, cg_ref, bu_ref, bg_ref, o_ref,
                   wbf_ref, up_ref, *, tiles_per_batch):
    i = pl.program_id(1)
    tm, tn = o_ref.shape
    pad = 8

    @pl.when(i == 0)
    def _():
        wbf_ref[:, :tn] = wu_ref[...].astype(BF16)
        wbf_ref[:, tn:] = wg_ref[...].astype(BF16)

    @pl.when(i % tiles_per_batch == 0)
    def _():
        up_ref[0:pad, :] = jnp.zeros((pad, 2 * tn), F32)

    @pl.when(i % tiles_per_batch != 0)
    def _():
        up_ref[0:pad, :] = up_ref[tm:tm + pad, :]

    up_ref[pad:pad + tm, :] = _dot(h_ref[...], wbf_ref[...])

    def conv(lo, w_ref, b_ref):
        cols = slice(lo, lo + tn)
        acc = b_ref[...] + w_ref[CONV_WIDTH - 1:CONV_WIDTH, :] * up_ref[pad:pad + tm, cols]
        for back in range(1, CONV_WIDTH):
            tap = CONV_WIDTH - 1 - back
            acc = acc + w_ref[tap:tap + 1, :] * up_ref[pad - back:pad - back + tm, cols]
        return acc

    u = conv(0, cu_ref, bu_ref)
    g = conv(tn, cg_ref, bg_ref)
    gelu = 0.5 * g * (1.0 + jnp.tanh(np.sqrt(2.0 / np.pi) * (g + 0.044715 * (g * g * g))))
    o_ref[...] = (gelu * u).astype(o_ref.dtype)


def _ffn_up(h, w_up, w_conv, b_conv3, l, seq):
    m, d = h.shape
    f = w_up.shape[-1] // 2
    tm, tn = 512, 512
    nt = f // tn
    kern = functools.partial(_ffn_up_kernel, tiles_per_batch=seq // tm)
    return pl.pallas_call(
        kern,
        grid=(nt, m // tm),
        in_specs=[pl.BlockSpec((tm, d), lambda j, i: (i, 0)),
                  pl.BlockSpec((None, d, tn), lambda j, i: (l, 0, j)),
                  pl.BlockSpec((None, d, tn), lambda j, i: (l, 0, nt + j)),
                  pl.BlockSpec((None, CONV_WIDTH, tn), lambda j, i: (l, 0, j)),
                  pl.BlockSpec((None, CONV_WIDTH, tn), lambda j, i: (l, 0, nt + j)),
                  pl.BlockSpec((None, 1, tn), lambda j, i: (l, 0, j)),
                  pl.BlockSpec((None, 1, tn), lambda j, i: (l, 0, nt + j))],
        out_specs=pl.BlockSpec((tm, tn), lambda j, i: (i, j)),
        out_shape=jax.ShapeDtypeStruct((m, f), BF16),
        scratch_shapes=[pltpu.VMEM((d, 2 * tn), BF16),
                        pltpu.VMEM((tm + 8, 2 * tn), F32)],
        compiler_params=_params("arbitrary", "arbitrary"),
        name="ffn_up_conv_gate",
    )(h, w_up, w_up, w_conv, w_conv, b_conv3, b_conv3)


def kernel(x, c, w_ada, b_ada, g_pre_mix, w_in, w_alpha_up, b_alpha, g_gla_head, w_gla_out,
           w_moba_out, w_o, g_post_mix, g_pre_ffn, w_up, w_conv, b_conv, w_down, g_post_ffn):
    batch, seq, d = x.shape
    depth = w_ada.shape[0]
    m = batch * seq
    gla_k = w_alpha_up.shape[-1]
    gla_heads = gla_k // GLA_DK
    gla_dv = g_gla_head.shape[-1]
    gla_v = w_gla_out.shape[1]
    moba_w = w_moba_out.shape[1]
    moba_heads = moba_w // MOBA_HD
    low_col = 2 * gla_k + 2 * gla_v
    assert gla_v == gla_heads * gla_dv and d % IN_TILE == 0
    assert low_col % IN_TILE == 0 and w_in.shape[-1] == low_col + GLA_RANK + 3 * moba_w + 2 * d
    first_shifted = low_col // IN_TILE
    n_tiles = first_shifted + 3 * moba_w // IN_TILE + 2 * d // IN_TILE

    x2 = x.reshape(m, d)
    c8 = jnp.pad(c, ((0, 8 - batch), (0, 0)))
    as3 = lambda a: a.reshape(a.shape[0], 1, a.shape[-1])
    wup_pad = jnp.pad(w_alpha_up, ((0, 0), (0, V7X_LANES - GLA_RANK), (0, 0)))

    for l in range(depth):
        mod = _modulation(c8, w_ada, as3(b_ada), l)
        mod3 = mod[:batch].reshape(batch, N_MOD, d)
        h = _prenorm(x2, as3(g_pre_mix), mod3, l, seq, shift_row=0, scale_row=1)
        proj = _in_projection(h, w_in, l, n_tiles, first_shifted)
        log_a = _log_alpha(h, w_in, wup_pad, as3(b_alpha), l, low_col // V7X_LANES)
        o_gla = _gla(proj, log_a, as3(g_gla_head), l, batch, seq, gla_heads, gla_dv)
        o_moba = _moba(proj, batch, seq, moba_heads)
        merged = _merge(o_gla, o_moba, w_gla_out, w_moba_out, proj, l, d)
        y = _matmul(merged, w_o, l, 1024, 1024, "out_projection")
        x2, h2 = _post(x2, y, as3(g_post_mix), mod3, l, seq, gate_row=2,
                       gpre3=as3(g_pre_ffn), next_rows=(3, 4))
        act = _ffn_up(h2, w_up, w_conv, as3(b_conv), l, seq)
        y2 = _matmul(act, w_down, l, 512, 512, "ffn_down")
        x2 = _post(x2, y2, as3(g_post_ffn), mod3, l, seq, gate_row=5)
    return x2.reshape(batch, seq, d)
```

```python
import functools

import numpy as np
import jax
import jax.numpy as jnp
from jax import lax
from jax.experimental import pallas as pl
from jax.experimental.pallas import tpu as pltpu

F32 = jnp.float32
BF16 = jnp.bfloat16

GLA_DK = 128
GLA_RANK = 16
GLA_GATE_TEMP = 16.0
GLA_CHUNK = 64
MOBA_HD = 128
MOBA_BLOCK = 256
MOBA_TOPK = 3
MOBA_HEADS_PER_STEP = 4
CONV_WIDTH = 3
N_MOD = 6
EPS = 1e-6

V7X_LANES = 128
V7X_VMEM_LIMIT_BYTES = 56 * 1024 * 1024

IN_TILE = 1024
MASK_VALUE = -1e30
LOG2E = 1.4426950408889634
FFN_HEAD_ROWS = 8


def _params(*semantics):
    return pltpu.CompilerParams(dimension_semantics=semantics,
                                vmem_limit_bytes=V7X_VMEM_LIMIT_BYTES)


def _dot(a, b):
    return jnp.dot(a, b, preferred_element_type=F32)


def _dot_nt(a, b):
    return lax.dot_general(a, b, (((1,), (1,)), ((), ())), preferred_element_type=F32)


def _dot_tn(a, b):
    return lax.dot_general(a, b, (((0,), (0,)), ((), ())), preferred_element_type=F32)


def _split3(x):
    x1 = x.astype(BF16)
    r1 = x - x1.astype(F32)
    x2 = r1.astype(BF16)
    x3 = (r1 - x2.astype(F32)).astype(BF16)
    return x1, x2, x3


def _rms(x, gain):
    return x * lax.rsqrt(jnp.mean(x * x, axis=-1, keepdims=True) + EPS) * gain


def _mod_kernel(c_ref, w_ref, b_ref, o_ref):
    c = c_ref[...]
    s = c * jax.nn.sigmoid(c)
    o_ref[...] = _dot(s.astype(BF16), w_ref[...].astype(BF16)) + b_ref[...]


def _modulation(c8, w_ada, b_ada3, l):
    rows, d = c8.shape
    n = w_ada.shape[-1]
    tn = 1024
    return pl.pallas_call(
        _mod_kernel,
        grid=(n // tn,),
        in_specs=[pl.BlockSpec((rows, d), lambda j: (0, 0)),
                  pl.BlockSpec((None, d, tn), lambda j: (l, 0, j)),
                  pl.BlockSpec((None, 1, tn), lambda j: (l, 0, j))],
        out_specs=pl.BlockSpec((rows, tn), lambda j: (0, j)),
        out_shape=jax.ShapeDtypeStruct((rows, n), F32),
        compiler_params=_params("arbitrary"),
        name="adaln_mod",
    )(c8, w_ada, b_ada3)


def _prenorm_kernel(x_ref, g_ref, mod_ref, o_ref, *, shift_row, scale_row):
    y = _rms(x_ref[...], g_ref[...])
    sh = mod_ref[shift_row:shift_row + 1, :]
    sc = mod_ref[scale_row:scale_row + 1, :]
    o_ref[...] = (y * (1.0 + sc) + sh).astype(o_ref.dtype)


def _prenorm(x2, gain3, mod3, l, seq, shift_row, scale_row):
    m, d = x2.shape
    tm = 512
    per_batch = seq // tm
    return pl.pallas_call(
        functools.partial(_prenorm_kernel, shift_row=shift_row, scale_row=scale_row),
        grid=(m // tm,),
        in_specs=[pl.BlockSpec((tm, d), lambda i: (i, 0)),
                  pl.BlockSpec((None, 1, d), lambda i: (l, 0, 0)),
                  pl.BlockSpec((None, N_MOD, d), lambda i: (i // per_batch, 0, 0))],
        out_specs=pl.BlockSpec((tm, d), lambda i: (i, 0)),
        out_shape=jax.ShapeDtypeStruct((m, d), BF16),
        compiler_params=_params("arbitrary"),
        name="prenorm_mix",
    )(x2, gain3, mod3)


def _inproj_kernel(h_ref, wa_ref, wb_ref, o_ref, wbf_ref, *, first_shifted):
    j = pl.program_id(0)
    i = pl.program_id(1)
    keep = IN_TILE - GLA_RANK

    @pl.when(jnp.logical_and(i == 0, j < first_shifted))
    def _():
        wbf_ref[...] = wa_ref[...].astype(BF16)

    @pl.when(jnp.logical_and(i == 0, j >= first_shifted))
    def _():
        wbf_ref[:keep, :] = wa_ref[GLA_RANK:, :].astype(BF16)
        wbf_ref[keep:, :] = wb_ref[...].astype(BF16)

    o_ref[...] = _dot_nt(h_ref[...], wbf_ref[...]).astype(o_ref.dtype)


def _in_projection(h, w_in_t, l, n_tiles, first_shifted):
    m, d = h.shape
    tm = 1024
    rank_blocks = IN_TILE // GLA_RANK
    return pl.pallas_call(
        functools.partial(_inproj_kernel, first_shifted=first_shifted),
        grid=(n_tiles, m // tm),
        in_specs=[pl.BlockSpec((tm, d), lambda j, i: (i, 0)),
                  pl.BlockSpec((None, IN_TILE, d), lambda j, i: (l, j, 0)),
                  pl.BlockSpec((None, GLA_RANK, d), lambda j, i: (l, (j + 1) * rank_blocks, 0))],
        out_specs=pl.BlockSpec((None, tm, IN_TILE), lambda j, i: (j, i, 0)),
        out_shape=jax.ShapeDtypeStruct((n_tiles, m, IN_TILE), BF16),
        scratch_shapes=[pltpu.VMEM((IN_TILE, d), BF16)],
        compiler_params=_params("arbitrary", "arbitrary"),
        name="in_projection",
    )(h, w_in_t, w_in_t)


def _loga_kernel(h_ref, w_ref, wup_ref, b_ref, o_ref):
    low = _dot_nt(h_ref[...], w_ref[...].astype(BF16))
    z = _dot(low.astype(BF16), wup_ref[...].astype(BF16)) + b_ref[...]
    log_sig = jnp.minimum(z, 0.0) - jnp.log1p(jnp.exp(-jnp.abs(z)))
    o_ref[...] = log_sig / GLA_GATE_TEMP


def _log_alpha(h, w_in_t, wup_pad, b_alpha3, l, low_block):
    m, d = h.shape
    k = wup_pad.shape[-1]
    tm = 1024
    return pl.pallas_call(
        _loga_kernel,
        grid=(m // tm,),
        in_specs=[pl.BlockSpec((tm, d), lambda i: (i, 0)),
                  pl.BlockSpec((None, V7X_LANES, d), lambda i: (l, low_block, 0)),
                  pl.BlockSpec((None, V7X_LANES, k), lambda i: (l, 0, 0)),
                  pl.BlockSpec((None, 1, k), lambda i: (l, 0, 0))],
        out_specs=pl.BlockSpec((tm, k), lambda i: (i, 0)),
        out_shape=jax.ShapeDtypeStruct((m, k), F32),
        compiler_params=_params("arbitrary"),
        name="gla_log_alpha",
    )(h, w_in_t, wup_pad, b_alpha3)


def _gla_tables(c):
    levels = int(np.log2(c))
    t = np.arange(c)
    tabs = [(t[None, :] <= t[:, None]),
            (t[None, :] > t[:, None])]
    for lv in range(1, levels + 1):
        size, half = 1 << lv, 1 << (lv - 1)
        mid = (t // size) * size + half
        second = t >= mid
        q_side = second[:, None] & (t[None, :] >= mid[:, None]) & (t[None, :] <= t[:, None])
        k_side = (~second)[:, None] & (t[None, :] > t[:, None]) & (t[None, :] < mid[:, None])
        tabs.append(q_side | k_side)
    e = np.concatenate(tabs, axis=0).astype(np.float32)
    x = t[:, None] ^ t[None, :]
    lvl = np.where(x == 0, 0, np.floor(np.log2(np.maximum(x, 1))).astype(np.int64) + 1)
    lvl = np.where(t[:, None] >= t[None, :], lvl, -1).astype(np.int32)
    return e, lvl, levels


def _gla_kernel(q_ref, k_ref, v_ref, r_ref, la_ref, gh_ref, e_ref, lvl_ref, o_ref, st_ref,
                *, heads, dv, levels, n_chunks):
    c = GLA_CHUNK
    dk = GLA_DK

    @pl.when(pl.program_id(1) == 0)
    def _():
        st_ref[...] = jnp.zeros_like(st_ref)

    e_tab = e_ref[...]
    lvl = lvl_ref[...]
    row = lax.broadcasted_iota(jnp.int32, (c, 1), 0)
    gain = gh_ref[...]

    def chunk(ci, carry):
        rows = pl.ds(pl.multiple_of(ci * c, c), c)
        g1, g2, g3 = _split3(la_ref[rows, :])
        ex = jnp.exp(_dot(e_tab, g1) + _dot(e_tab, g2) + _dot(e_tab, g3))
        ex_b = ex[0:c]
        ex_last = ex[c:2 * c]
        q_all = q_ref[rows, :].astype(F32) * (dk ** -0.5)
        k_all = k_ref[rows, :].astype(F32)
        for h in range(heads):
            sl = slice(h * dk, (h + 1) * dk)
            vs = slice(h * dv, (h + 1) * dv)
            qh, kh = q_all[:, sl], k_all[:, sl]
            vh = v_ref[rows, vs]
            st = st_ref[h]
            o = _dot_nt((qh * ex_b[:, sl]).astype(BF16), st.astype(BF16))
            a = jnp.where(lvl == 0, _dot_nt(qh.astype(BF16), kh.astype(BF16)), 0.0)
            for lv in range(1, levels + 1):
                e_lv = ex[(1 + lv) * c:(2 + lv) * c, sl]
                second = ((row >> (lv - 1)) & 1) == 1
                ql = jnp.where(second, qh * e_lv, 0.0).astype(BF16)
                kl = jnp.where(second, 0.0, kh * e_lv).astype(BF16)
                a = a + jnp.where(lvl == lv, _dot_nt(ql, kl), 0.0)
            o = o + _dot(a.astype(BF16), vh)
            kt = (kh * ex_last[:, sl]).astype(BF16)
            st_ref[h] = st * ex_b[c - 1:c, sl] + _dot_tn(vh, kt)
            y = _rms(o, gain)
            r = r_ref[rows, vs].astype(F32)
            o_ref[rows, vs] = (y * (r * jax.nn.sigmoid(r))).astype(o_ref.dtype)
        return carry

    lax.fori_loop(0, n_chunks, chunk, 0)


def _gla(proj, log_a, g_head3, l, batch, seq, heads, dv):
    n_t, m, _ = proj.shape
    dk = GLA_DK
    rows = 512
    per_batch = seq // rows
    e_np, lvl_np, levels = _gla_tables(GLA_CHUNK)
    e_tab = jnp.asarray(e_np, BF16)
    lvl = jnp.asarray(lvl_np)
    kw = heads * dk
    assert 2 * kw == IN_TILE and heads * dv == IN_TILE
    kern = functools.partial(_gla_kernel, heads=heads, dv=dv, levels=levels,
                             n_chunks=rows // GLA_CHUNK)
    return pl.pallas_call(
        kern,
        grid=(batch, per_batch),
        in_specs=[pl.BlockSpec((None, rows, kw), lambda b, r: (0, b * per_batch + r, 0)),
                  pl.BlockSpec((None, rows, kw), lambda b, r: (0, b * per_batch + r, 1)),
                  pl.BlockSpec((None, rows, IN_TILE), lambda b, r: (1, b * per_batch + r, 0)),
                  pl.BlockSpec((None, rows, IN_TILE), lambda b, r: (2, b * per_batch + r, 0)),
                  pl.BlockSpec((rows, kw), lambda b, r: (b * per_batch + r, 0)),
                  pl.BlockSpec((None, 1, dv), lambda b, r: (l, 0, 0)),
                  pl.BlockSpec(e_np.shape, lambda b, r: (0, 0)),
                  pl.BlockSpec(lvl_np.shape, lambda b, r: (0, 0))],
        out_specs=pl.BlockSpec((rows, IN_TILE), lambda b, r: (b * per_batch + r, 0)),
        out_shape=jax.ShapeDtypeStruct((m, IN_TILE), BF16),
        scratch_shapes=[pltpu.VMEM((heads, dv, dk), F32)],
        compiler_params=_params("arbitrary", "arbitrary"),
        name="gla",
    )(proj, proj, proj, proj, log_a, g_head3, e_tab, lvl)


def _moba_kernel(q_ref, k_ref, v_ref, o_ref, kmean_ref, vt_ref, bias_ref, mblk_ref, causal_ref,
                 s_ref, m_ref, l_ref, acc_ref, *, n_blocks, topk, heads):
    blk, hd = MOBA_BLOCK, MOBA_HD
    i = pl.program_id(2)
    c2 = (hd ** -0.5) * LOG2E
    head_lanes = [slice(h * hd, (h + 1) * hd) for h in range(heads)]

    @pl.when(i == 0)
    def _():
        for h, lanes in enumerate(head_lanes):
            for n in range(n_blocks):
                rows = slice(n * blk, (n + 1) * blk)
                kmean_ref[h, n:n + 1, :] = jnp.mean(k_ref[rows, lanes].astype(F32), axis=0, keepdims=True)
                vt_ref[h, :, rows] = v_ref[rows, lanes].astype(F32).T.astype(BF16)

        mblk_ref[...] = jnp.zeros_like(mblk_ref)
        kpos = lax.broadcasted_iota(jnp.int32, (blk, blk), 0)
        qpos = lax.broadcasted_iota(jnp.int32, (blk, blk), 1)
        causal_ref[...] = jnp.where(kpos <= qpos, 0.0, MASK_VALUE)

    for h, lanes in enumerate(head_lanes):
        q = q_ref[:, lanes]
        m1, m2, m3 = _split3(kmean_ref[h])
        s_blk = _dot_nt(m1, q) + _dot_nt(m2, q) + _dot_nt(m3, q)
        bidx = lax.broadcasted_iota(jnp.int32, s_blk.shape, 0)
        cur = jnp.where(bidx < i, s_blk, MASK_VALUE)
        sel = jnp.zeros(s_blk.shape, jnp.bool_)
        for _ in range(topk):
            mx = jnp.max(cur, axis=0, keepdims=True)
            idx = jnp.min(jnp.where(cur == mx, bidx, n_blocks), axis=0, keepdims=True)
            pick = bidx == idx
            sel = jnp.logical_or(sel, jnp.logical_and(pick, idx < i))
            cur = jnp.where(pick, -jnp.inf, cur)
        bias_ref[h] = jnp.where(jnp.logical_or(sel, bidx == i), 0.0, MASK_VALUE)

    def pairwise(n, body):
        def two(t, carry):
            body(2 * t)
            body(2 * t + 1)
            return carry

        lax.fori_loop(0, lax.shift_right_logical(n, 1), two, 0)

        @pl.when(n % 2 == 1)
        def _():
            body(n - 1)

    def score(j, causal):
        cols = pl.ds(pl.multiple_of(j * blk, blk), blk)
        for h, lanes in enumerate(head_lanes):
            s = _dot_nt(k_ref[cols, lanes], q_ref[:, lanes]) * c2
            if causal:
                s = s + causal_ref[...]
            s_ref[h, j] = s
            mblk_ref[h, pl.ds(j, 1), :] = jnp.max(s, axis=0, keepdims=True)

    pairwise(i, lambda j: score(j, False))
    score(i, True)

    for h in range(heads):
        bidx = lax.broadcasted_iota(jnp.int32, (n_blocks, blk), 0)
        allowed_max = jnp.where(bidx <= i, mblk_ref[h] + bias_ref[h], -jnp.inf)
        m_ref[h] = jnp.max(allowed_max, axis=0, keepdims=True)
        l_ref[h] = jnp.zeros((1, blk), F32)
        acc_ref[h] = jnp.zeros((hd, blk), F32)

    def attend(j):
        cols = pl.ds(pl.multiple_of(j * blk, blk), blk)
        for h in range(heads):
            p = jnp.exp2(s_ref[h, j] + (bias_ref[h, pl.ds(j, 1), :] - m_ref[h]))
            l_ref[h] += jnp.sum(p, axis=0, keepdims=True)
            acc_ref[h] += _dot(vt_ref[h, :, cols], p.astype(BF16))

    pairwise(i + 1, attend)
    for h, lanes in enumerate(head_lanes):
        o_ref[:, lanes] = (acc_ref[h] / l_ref[h]).T.astype(o_ref.dtype)


def _moba(proj, batch, seq, heads):
    n_t, m, _ = proj.shape
    hd, blk = MOBA_HD, MOBA_BLOCK
    hp = MOBA_HEADS_PER_STEP
    assert seq % blk == 0 and heads * hd == IN_TILE and heads % hp == 0
    n_blocks = seq // blk
    topk = min(MOBA_TOPK, n_blocks - 1)
    wide = hp * hd
    kern = functools.partial(_moba_kernel, n_blocks=n_blocks, topk=topk, heads=hp)
    return pl.pallas_call(
        kern,
        grid=(batch, heads // hp, n_blocks),
        in_specs=[pl.BlockSpec((None, blk, wide), lambda b, g, i: (3, b * n_blocks + i, g)),
                  pl.BlockSpec((None, seq, wide), lambda b, g, i: (4, b, g)),
                  pl.BlockSpec((None, seq, wide), lambda b, g, i: (5, b, g))],
        out_specs=pl.BlockSpec((blk, wide), lambda b, g, i: (b * n_blocks + i, g)),
        out_shape=jax.ShapeDtypeStruct((m, IN_TILE), BF16),
        scratch_shapes=[pltpu.VMEM((hp, n_blocks, hd), F32),
                        pltpu.VMEM((hp, hd, seq), BF16),
                        pltpu.VMEM((hp, n_blocks, blk), F32),
                        pltpu.VMEM((hp, n_blocks, blk), F32),
                        pltpu.VMEM((blk, blk), F32),
                        pltpu.VMEM((hp, n_blocks, blk, blk), F32),
                        pltpu.VMEM((hp, 1, blk), F32),
                        pltpu.VMEM((hp, 1, blk), F32),
                        pltpu.VMEM((hp, hd, blk), F32)],
        compiler_params=_params("arbitrary", "arbitrary", "arbitrary"),
        name="moba",
    )(proj, proj, proj)


def _merge_kernel(og_ref, om_ref, wg_ref, wm_ref, ga_ref, gb_ref, o_ref, wgb_ref, wmb_ref):
    @pl.when(pl.program_id(1) == 0)
    def _():
        wgb_ref[...] = wg_ref[...].astype(BF16)
        wmb_ref[...] = wm_ref[...].astype(BF16)

    ya = _dot(og_ref[...], wgb_ref[...])
    yb = _dot(om_ref[...], wmb_ref[...])
    ga = jax.nn.sigmoid(ga_ref[...].astype(F32))
    gb = jax.nn.sigmoid(gb_ref[...].astype(F32))
    o_ref[...] = (ga * ya + gb * yb).astype(o_ref.dtype)


def _merge(o_gla, o_moba, w_gla_out, w_moba_out, proj, l, d, gate_a0):
    m, kg = o_gla.shape
    km = o_moba.shape[1]
    tm, tn = 1024, IN_TILE
    n_tiles = d // tn
    gate_b0 = gate_a0 + n_tiles
    return pl.pallas_call(
        _merge_kernel,
        grid=(n_tiles, m // tm),
        in_specs=[pl.BlockSpec((tm, kg), lambda j, i: (i, 0)),
                  pl.BlockSpec((tm, km), lambda j, i: (i, 0)),
                  pl.BlockSpec((None, kg, tn), lambda j, i: (l, 0, j)),
                  pl.BlockSpec((None, km, tn), lambda j, i: (l, 0, j)),
                  pl.BlockSpec((None, tm, tn), lambda j, i: (gate_a0 + j, i, 0)),
                  pl.BlockSpec((None, tm, tn), lambda j, i: (gate_b0 + j, i, 0))],
        out_specs=pl.BlockSpec((tm, tn), lambda j, i: (i, j)),
        out_shape=jax.ShapeDtypeStruct((m, d), BF16),
        scratch_shapes=[pltpu.VMEM((kg, tn), BF16), pltpu.VMEM((km, tn), BF16)],
        compiler_params=_params("arbitrary", "arbitrary"),
        name="branch_merge",
    )(o_gla, o_moba, w_gla_out, w_moba_out, proj, proj)


def _matmul_kernel(a_ref, w_ref, o_ref, wbf_ref):
    @pl.when(pl.program_id(1) == 0)
    def _():
        wbf_ref[...] = w_ref[...].astype(BF16)

    o_ref[...] = _dot(a_ref[...], wbf_ref[...]).astype(o_ref.dtype)


def _matmul(a, w, l, tm, tn, name):
    m, k = a.shape
    n = w.shape[-1]
    return pl.pallas_call(
        _matmul_kernel,
        grid=(n // tn, m // tm),
        in_specs=[pl.BlockSpec((tm, k), lambda j, i: (i, 0)),
                  pl.BlockSpec((None, k, tn), lambda j, i: (l, 0, j))],
        out_specs=pl.BlockSpec((tm, tn), lambda j, i: (i, j)),
        out_shape=jax.ShapeDtypeStruct((m, n), F32),
        scratch_shapes=[pltpu.VMEM((k, tn), BF16)],
        compiler_params=_params("arbitrary", "arbitrary"),
        name=name,
    )(a, w)


def _post_kernel(x_ref, y_ref, gpost_ref, mod_ref, *rest, gate_row, next_rows):
    x = x_ref[...] + mod_ref[gate_row:gate_row + 1, :] * _rms(y_ref[...], gpost_ref[...])
    if next_rows is None:
        (o_ref,) = rest
        o_ref[...] = x
    else:
        gpre_ref, o_ref, h_ref = rest
        shift_row, scale_row = next_rows
        o_ref[...] = x
        h = _rms(x, gpre_ref[...])
        h_ref[...] = (h * (1.0 + mod_ref[scale_row:scale_row + 1, :])
                      + mod_ref[shift_row:shift_row + 1, :]).astype(h_ref.dtype)


def _post(x2, y, gpost3, mod3, l, seq, gate_row, gpre3=None, next_rows=None):
    m, d = x2.shape
    tm = 512
    per_batch = seq // tm
    row_spec = pl.BlockSpec((tm, d), lambda i: (i, 0))
    gain_spec = pl.BlockSpec((None, 1, d), lambda i: (l, 0, 0))
    in_specs = [row_spec, row_spec, gain_spec,
                pl.BlockSpec((None, N_MOD, d), lambda i: (i // per_batch, 0, 0))]
    args = [x2, y, gpost3, mod3]
    if next_rows is None:
        out_specs, out_shape = row_spec, jax.ShapeDtypeStruct((m, d), F32)
    else:
        in_specs.append(gain_spec)
        args.append(gpre3)
        out_specs = (row_spec, row_spec)
        out_shape = (jax.ShapeDtypeStruct((m, d), F32), jax.ShapeDtypeStruct((m, d), BF16))
    return pl.pallas_call(
        functools.partial(_post_kernel, gate_row=gate_row, next_rows=next_rows),
        grid=(m // tm,),
        in_specs=in_specs,
        out_specs=out_specs,
        out_shape=out_shape,
        compiler_params=_params("arbitrary"),
        name="post_norm_residual",
    )(*args)


def _ffn_up_kernel(h_ref, wu_ref, wg_ref, cu_ref, cg_ref, bu_ref, bg_ref, o_ref,
                   wbf_ref, up_ref, *, tiles_per_batch):
    i = pl.program_id(1)
    tm, tn = o_ref.shape
    pad = FFN_HEAD_ROWS

    @pl.when(i == 0)
    def _():
        wbf_ref[:, :tn] = wu_ref[...].astype(BF16)
        wbf_ref[:, tn:] = wg_ref[...].astype(BF16)

    @pl.when(i % tiles_per_batch == 0)
    def _():
        up_ref[0:pad, :] = jnp.zeros((pad, 2 * tn), F32)

    @pl.when(i % tiles_per_batch != 0)
    def _():
        up_ref[0:pad, :] = up_ref[tm:tm + pad, :]

    up_ref[pad:pad + tm, :] = _dot(h_ref[...], wbf_ref[...])

    def conv(lo, w_ref, b_ref):
        cols = slice(lo, lo + tn)
        acc = b_ref[...] + w_ref[CONV_WIDTH - 1:CONV_WIDTH, :] * up_ref[pad:pad + tm, cols]
        for back in range(1, CONV_WIDTH):
            tap = CONV_WIDTH - 1 - back
            acc = acc + w_ref[tap:tap + 1, :] * up_ref[pad - back:pad - back + tm, cols]
        return acc

    u = conv(0, cu_ref, bu_ref)
    g = conv(tn, cg_ref, bg_ref)
    gelu = 0.5 * g * (1.0 + jnp.tanh(np.sqrt(2.0 / np.pi) * (g + 0.044715 * (g * g * g))))
    o_ref[...] = (gelu * u).astype(o_ref.dtype)


def _ffn_up(h, w_up, w_conv, b_conv3, l, seq):
    m, d = h.shape
    f = w_up.shape[-1] // 2
    tm, tn = 512, 512
    nt = f // tn
    kern = functools.partial(_ffn_up_kernel, tiles_per_batch=seq // tm)
    return pl.pallas_call(
        kern,
        grid=(nt, m // tm),
        in_specs=[pl.BlockSpec((tm, d), lambda j, i: (i, 0)),
                  pl.BlockSpec((None, d, tn), lambda j, i: (l, 0, j)),
                  pl.BlockSpec((None, d, tn), lambda j, i: (l, 0, nt + j)),
                  pl.BlockSpec((None, CONV_WIDTH, tn), lambda j, i: (l, 0, j)),
                  pl.BlockSpec((None, CONV_WIDTH, tn), lambda j, i: (l, 0, nt + j)),
                  pl.BlockSpec((None, 1, tn), lambda j, i: (l, 0, j)),
                  pl.BlockSpec((None, 1, tn), lambda j, i: (l, 0, nt + j))],
        out_specs=pl.BlockSpec((tm, tn), lambda j, i: (i, j)),
        out_shape=jax.ShapeDtypeStruct((m, f), BF16),
        scratch_shapes=[pltpu.VMEM((d, 2 * tn), BF16),
                        pltpu.VMEM((tm + FFN_HEAD_ROWS, 2 * tn), F32)],
        compiler_params=_params("arbitrary", "arbitrary"),
        name="ffn_up_conv_gate",
    )(h, w_up, w_up, w_conv, w_conv, b_conv3, b_conv3)


def kernel(x, c, w_ada, b_ada, g_pre_mix, w_in, w_alpha_up, b_alpha, g_gla_head, w_gla_out,
           w_moba_out, w_o, g_post_mix, g_pre_ffn, w_up, w_conv, b_conv, w_down, g_post_ffn):
    batch, seq, d = x.shape
    depth = w_ada.shape[0]
    m = batch * seq
    gla_k = w_alpha_up.shape[-1]
    gla_heads = gla_k // GLA_DK
    gla_dv = g_gla_head.shape[-1]
    gla_v = w_gla_out.shape[1]
    moba_w = w_moba_out.shape[1]
    moba_heads = moba_w // MOBA_HD
    low_col = 2 * gla_k + 2 * gla_v
    assert gla_v == gla_heads * gla_dv and d % IN_TILE == 0
    assert low_col % IN_TILE == 0 and w_in.shape[-1] == low_col + GLA_RANK + 3 * moba_w + 2 * d
    first_shifted = low_col // IN_TILE
    n_tiles = first_shifted + 3 * moba_w // IN_TILE + 2 * d // IN_TILE

    x2 = x.reshape(m, d)
    c8 = jnp.pad(c, ((0, 8 - batch), (0, 0)))
    as3 = lambda a: a.reshape(a.shape[0], 1, a.shape[-1])
    wup_pad = jnp.pad(w_alpha_up, ((0, 0), (0, V7X_LANES - GLA_RANK), (0, 0)))
    w_in_t = jnp.swapaxes(w_in, 1, 2)

    for l in range(depth):
        mod = _modulation(c8, w_ada, as3(b_ada), l)
        mod3 = mod[:batch].reshape(batch, N_MOD, d)
        h = _prenorm(x2, as3(g_pre_mix), mod3, l, seq, shift_row=0, scale_row=1)
        proj = _in_projection(h, w_in_t, l, n_tiles, first_shifted)
        log_a = _log_alpha(h, w_in_t, wup_pad, as3(b_alpha), l, low_col // V7X_LANES)
        o_gla = _gla(proj, log_a, as3(g_gla_head), l, batch, seq, gla_heads, gla_dv)
        o_moba = _moba(proj, batch, seq, moba_heads)
        merged = _merge(o_gla, o_moba, w_gla_out, w_moba_out, proj, l, d,
                        gate_a0=first_shifted + 3 * moba_w // IN_TILE)
        y = _matmul(merged, w_o, l, 1024, 1024, "out_projection")
        x2, h2 = _post(x2, y, as3(g_post_mix), mod3, l, seq, gate_row=2,
                       gpre3=as3(g_pre_ffn), next_rows=(3, 4))
        act = _ffn_up(h2, w_up, w_conv, as3(b_conv), l, seq)
        y2 = _matmul(act, w_down, l, 512, 512, "ffn_down")
        x2 = _post(x2, y2, as3(g_post_ffn), mod3, l, seq, gate_row=5)
    return x2.reshape(batch, seq, d)
```

```python
import functools

import numpy as np
import jax
import jax.numpy as jnp
from jax import lax
from jax.experimental import pallas as pl
from jax.experimental.pallas import tpu as pltpu

F32 = jnp.float32
BF16 = jnp.bfloat16

GLA_DK = 128
GLA_RANK = 16
GLA_GATE_TEMP = 16.0
GLA_CHUNK = 64
MOBA_HD = 128
MOBA_BLOCK = 256
MOBA_TOPK = 3
MOBA_HEADS_PER_STEP = 4
CONV_WIDTH = 3
N_MOD = 6
EPS = 1e-6

V7X_LANES = 128
V7X_SUBLANES = 8
V7X_VMEM_LIMIT_BYTES = 56 * 1024 * 1024

IN_TILE = 1024
MASK_VALUE = -1e30
LOG2E = 1.4426950408889634
FFN_TILE = 512
FFN_ROW_CHUNK = 256
FFN_ROW_GROUP = 16


def _params(*semantics):
    return pltpu.CompilerParams(dimension_semantics=semantics,
                                vmem_limit_bytes=V7X_VMEM_LIMIT_BYTES)


def _dot(a, b):
    return jnp.dot(a, b, preferred_element_type=F32)


def _dot_nt(a, b):
    return lax.dot_general(a, b, (((1,), (1,)), ((), ())), preferred_element_type=F32)


def _dot_tn(a, b):
    return lax.dot_general(a, b, (((0,), (0,)), ((), ())), preferred_element_type=F32)


def _split3(x):
    x1 = x.astype(BF16)
    r1 = x - x1.astype(F32)
    x2 = r1.astype(BF16)
    x3 = (r1 - x2.astype(F32)).astype(BF16)
    return x1, x2, x3


def _rms(x, gain):
    return x * lax.rsqrt(jnp.mean(x * x, axis=-1, keepdims=True) + EPS) * gain


def _mod_kernel(c_ref, w_ref, b_ref, o_ref):
    c = c_ref[...]
    s = c * jax.nn.sigmoid(c)
    o_ref[...] = _dot(s.astype(BF16), w_ref[...].astype(BF16)) + b_ref[...]


def _modulation(c8, w_ada, b_ada3, l):
    rows, d = c8.shape
    n = w_ada.shape[-1]
    tn = 1024
    return pl.pallas_call(
        _mod_kernel,
        grid=(n // tn,),
        in_specs=[pl.BlockSpec((rows, d), lambda j: (0, 0)),
                  pl.BlockSpec((None, d, tn), lambda j: (l, 0, j)),
                  pl.BlockSpec((None, 1, tn), lambda j: (l, 0, j))],
        out_specs=pl.BlockSpec((rows, tn), lambda j: (0, j)),
        out_shape=jax.ShapeDtypeStruct((rows, n), F32),
        compiler_params=_params("arbitrary"),
        name="adaln_mod",
    )(c8, w_ada, b_ada3)


def _prenorm_kernel(x_ref, g_ref, mod_ref, o_ref, *, shift_row, scale_row):
    y = _rms(x_ref[...], g_ref[...])
    sh = mod_ref[shift_row:shift_row + 1, :]
    sc = mod_ref[scale_row:scale_row + 1, :]
    o_ref[...] = (y * (1.0 + sc) + sh).astype(o_ref.dtype)


def _prenorm(x2, gain3, mod3, l, seq, shift_row, scale_row):
    m, d = x2.shape
    tm = 512
    per_batch = seq // tm
    return pl.pallas_call(
        functools.partial(_prenorm_kernel, shift_row=shift_row, scale_row=scale_row),
        grid=(m // tm,),
        in_specs=[pl.BlockSpec((tm, d), lambda i: (i, 0)),
                  pl.BlockSpec((None, 1, d), lambda i: (l, 0, 0)),
                  pl.BlockSpec((None, N_MOD, d), lambda i: (i // per_batch, 0, 0))],
        out_specs=pl.BlockSpec((tm, d), lambda i: (i, 0)),
        out_shape=jax.ShapeDtypeStruct((m, d), BF16),
        compiler_params=_params("arbitrary"),
        name="prenorm_mix",
    )(x2, gain3, mod3)


def _inproj_kernel(h_ref, wa_ref, wb_ref, o_ref, wbf_ref, *, first_shifted):
    j = pl.program_id(0)
    i = pl.program_id(1)
    keep = IN_TILE - GLA_RANK

    @pl.when(jnp.logical_and(i == 0, j < first_shifted))
    def _():
        wbf_ref[...] = wa_ref[...].astype(BF16)

    @pl.when(jnp.logical_and(i == 0, j >= first_shifted))
    def _():
        wbf_ref[:keep, :] = wa_ref[GLA_RANK:, :].astype(BF16)
        wbf_ref[keep:, :] = wb_ref[...].astype(BF16)

    o_ref[...] = _dot_nt(h_ref[...], wbf_ref[...]).astype(o_ref.dtype)


def _in_projection(h, w_in_t, l, n_tiles, first_shifted):
    m, d = h.shape
    tm = 1024
    rank_blocks = IN_TILE // GLA_RANK
    return pl.pallas_call(
        functools.partial(_inproj_kernel, first_shifted=first_shifted),
        grid=(n_tiles, m // tm),
        in_specs=[pl.BlockSpec((tm, d), lambda j, i: (i, 0)),
                  pl.BlockSpec((None, IN_TILE, d), lambda j, i: (l, j, 0)),
                  pl.BlockSpec((None, GLA_RANK, d), lambda j, i: (l, (j + 1) * rank_blocks, 0))],
        out_specs=pl.BlockSpec((None, tm, IN_TILE), lambda j, i: (j, i, 0)),
        out_shape=jax.ShapeDtypeStruct((n_tiles, m, IN_TILE), BF16),
        scratch_shapes=[pltpu.VMEM((IN_TILE, d), BF16)],
        compiler_params=_params("arbitrary", "arbitrary"),
        name="in_projection",
    )(h, w_in_t, w_in_t)


def _loga_kernel(h_ref, w_ref, wup_ref, b_ref, o_ref):
    low = _dot_nt(h_ref[...], w_ref[...].astype(BF16))
    z = _dot(low.astype(BF16), wup_ref[...].astype(BF16)) + b_ref[...]
    log_sig = jnp.minimum(z, 0.0) - jnp.log1p(jnp.exp(-jnp.abs(z)))
    o_ref[...] = log_sig / GLA_GATE_TEMP


def _log_alpha(h, w_in_t, wup_pad, b_alpha3, l, low_block):
    m, d = h.shape
    k = wup_pad.shape[-1]
    tm = 1024
    return pl.pallas_call(
        _loga_kernel,
        grid=(m // tm,),
        in_specs=[pl.BlockSpec((tm, d), lambda i: (i, 0)),
                  pl.BlockSpec((None, V7X_LANES, d), lambda i: (l, low_block, 0)),
                  pl.BlockSpec((None, V7X_LANES, k), lambda i: (l, 0, 0)),
                  pl.BlockSpec((None, 1, k), lambda i: (l, 0, 0))],
        out_specs=pl.BlockSpec((tm, k), lambda i: (i, 0)),
        out_shape=jax.ShapeDtypeStruct((m, k), F32),
        compiler_params=_params("arbitrary"),
        name="gla_log_alpha",
    )(h, w_in_t, wup_pad, b_alpha3)


def _gla_tables(c):
    levels = int(np.log2(c))
    t = np.arange(c)
    tabs = [(t[None, :] <= t[:, None]),
            (t[None, :] > t[:, None])]
    for lv in range(1, levels + 1):
        size, half = 1 << lv, 1 << (lv - 1)
        mid = (t // size) * size + half
        second = t >= mid
        q_side = second[:, None] & (t[None, :] >= mid[:, None]) & (t[None, :] <= t[:, None])
        k_side = (~second)[:, None] & (t[None, :] > t[:, None]) & (t[None, :] < mid[:, None])
        tabs.append(q_side | k_side)
    e = np.concatenate(tabs, axis=0).astype(np.float32)
    x = t[:, None] ^ t[None, :]
    lvl = np.where(x == 0, 0, np.floor(np.log2(np.maximum(x, 1))).astype(np.int64) + 1)
    lvl = np.where(t[:, None] >= t[None, :], lvl, -1).astype(np.int32)
    return e, lvl, levels


def _gla_kernel(q_ref, k_ref, v_ref, r_ref, la_ref, gh_ref, e_ref, lvl_ref, o_ref, st_ref,
                *, heads, dv, levels, n_chunks):
    c = GLA_CHUNK
    dk = GLA_DK

    @pl.when(pl.program_id(1) == 0)
    def _():
        st_ref[...] = jnp.zeros_like(st_ref)

    e_tab = e_ref[...]
    lvl = lvl_ref[...]
    row = lax.broadcasted_iota(jnp.int32, (c, 1), 0)
    gain = gh_ref[...]

    def chunk(ci, carry):
        rows = pl.ds(pl.multiple_of(ci * c, c), c)
        g3 = jnp.concatenate(_split3(la_ref[rows, :]), axis=0)
        ex = jnp.exp(_dot(e_tab, g3))
        ex_b = ex[0:c]
        ex_last = ex[c:2 * c]
        q_all = q_ref[rows, :].astype(F32) * (dk ** -0.5)
        k_all = k_ref[rows, :].astype(F32)
        for h in range(heads):
            sl = slice(h * dk, (h + 1) * dk)
            vs = slice(h * dv, (h + 1) * dv)
            qh, kh = q_all[:, sl], k_all[:, sl]
            vh = v_ref[rows, vs]
            st = st_ref[h]
            o = _dot_nt((qh * ex_b[:, sl]).astype(BF16), st.astype(BF16))
            a = jnp.where(lvl == 0, _dot_nt(qh.astype(BF16), kh.astype(BF16)), 0.0)
            for lv in range(1, levels + 1):
                e_lv = ex[(1 + lv) * c:(2 + lv) * c, sl]
                second = ((row >> (lv - 1)) & 1) == 1
                ql = jnp.where(second, qh * e_lv, 0.0).astype(BF16)
                kl = jnp.where(second, 0.0, kh * e_lv).astype(BF16)
                a = a + jnp.where(lvl == lv, _dot_nt(ql, kl), 0.0)
            o = o + _dot(a.astype(BF16), vh)
            kt = (kh * ex_last[:, sl]).astype(BF16)
            st_ref[h] = st * ex_b[c - 1:c, sl] + _dot_tn(vh, kt)
            y = _rms(o, gain)
            r = r_ref[rows, vs].astype(F32)
            o_ref[rows, vs] = (y * (r * jax.nn.sigmoid(r))).astype(o_ref.dtype)
        return carry

    lax.fori_loop(0, n_chunks, chunk, 0, unroll=2)


def _gla(proj, log_a, g_head3, l, batch, seq, heads, dv):
    n_t, m, _ = proj.shape
    dk = GLA_DK
    rows = 512
    per_batch = seq // rows
    e_np, lvl_np, levels = _gla_tables(GLA_CHUNK)
    e_np = np.concatenate([e_np, e_np, e_np], axis=1)
    e_tab = jnp.asarray(e_np, BF16)
    lvl = jnp.asarray(lvl_np)
    kw = heads * dk
    assert 2 * kw == IN_TILE and heads * dv == IN_TILE
    kern = functools.partial(_gla_kernel, heads=heads, dv=dv, levels=levels,
                             n_chunks=rows // GLA_CHUNK)
    return pl.pallas_call(
        kern,
        grid=(batch, per_batch),
        in_specs=[pl.BlockSpec((None, rows, kw), lambda b, r: (0, b * per_batch + r, 0)),
                  pl.BlockSpec((None, rows, kw), lambda b, r: (0, b * per_batch + r, 1)),
                  pl.BlockSpec((None, rows, IN_TILE), lambda b, r: (1, b * per_batch + r, 0)),
                  pl.BlockSpec((None, rows, IN_TILE), lambda b, r: (2, b * per_batch + r, 0)),
                  pl.BlockSpec((rows, kw), lambda b, r: (b * per_batch + r, 0)),
                  pl.BlockSpec((None, 1, dv), lambda b, r: (l, 0, 0)),
                  pl.BlockSpec(e_np.shape, lambda b, r: (0, 0)),
                  pl.BlockSpec(lvl_np.shape, lambda b, r: (0, 0))],
        out_specs=pl.BlockSpec((rows, IN_TILE), lambda b, r: (b * per_batch + r, 0)),
        out_shape=jax.ShapeDtypeStruct((m, IN_TILE), BF16),
        scratch_shapes=[pltpu.VMEM((heads, dv, dk), F32)],
        compiler_params=_params("arbitrary", "arbitrary"),
        name="gla",
    )(proj, proj, proj, proj, log_a, g_head3, e_tab, lvl)


def _moba_kernel(q_ref, k_ref, v_ref, o_ref, kmean_ref, vt_ref, bias_ref, mblk_ref, causal_ref,
                 s_ref, m_ref, l_ref, acc_ref, *, n_blocks, topk, heads):
    blk, hd = MOBA_BLOCK, MOBA_HD
    i = pl.program_id(2)
    c2 = (hd ** -0.5) * LOG2E
    head_lanes = [slice(h * hd, (h + 1) * hd) for h in range(heads)]

    @pl.when(i == 0)
    def _():
        for h, lanes in enumerate(head_lanes):
            for n in range(n_blocks):
                rows = slice(n * blk, (n + 1) * blk)
                kmean_ref[h, n:n + 1, :] = jnp.mean(k_ref[rows, lanes].astype(F32), axis=0, keepdims=True)
                vt_ref[h, :, rows] = v_ref[rows, lanes].astype(F32).T.astype(BF16)

        mblk_ref[...] = jnp.zeros_like(mblk_ref)
        kpos = lax.broadcasted_iota(jnp.int32, (blk, blk), 0)
        qpos = lax.broadcasted_iota(jnp.int32, (blk, blk), 1)
        causal_ref[...] = jnp.where(kpos <= qpos, 0.0, MASK_VALUE)

    for h, lanes in enumerate(head_lanes):
        q = q_ref[:, lanes]
        m1, m2, m3 = _split3(kmean_ref[h])
        s_blk = _dot_nt(m1, q) + _dot_nt(m2, q) + _dot_nt(m3, q)
        bidx = lax.broadcasted_iota(jnp.int32, s_blk.shape, 0)
        cur = jnp.where(bidx < i, s_blk, MASK_VALUE)
        sel = jnp.zeros(s_blk.shape, jnp.bool_)
        for _ in range(topk):
            mx = jnp.max(cur, axis=0, keepdims=True)
            idx = jnp.min(jnp.where(cur == mx, bidx, n_blocks), axis=0, keepdims=True)
            pick = bidx == idx
            sel = jnp.logical_or(sel, jnp.logical_and(pick, idx < i))
            cur = jnp.where(pick, -jnp.inf, cur)
        bias_ref[h] = jnp.where(jnp.logical_or(sel, bidx == i), 0.0, MASK_VALUE)

    def pairwise(n, body):
        def two(t, carry):
            body(2 * t)
            body(2 * t + 1)
            return carry

        lax.fori_loop(0, lax.shift_right_logical(n, 1), two, 0)

        @pl.when(n % 2 == 1)
        def _():
            body(n - 1)

    def score(j, causal):
        cols = pl.ds(pl.multiple_of(j * blk, blk), blk)
        for h, lanes in enumerate(head_lanes):
            s = _dot_nt(k_ref[cols, lanes], q_ref[:, lanes]) * c2
            if causal:
                s = s + causal_ref[...]
            s_ref[h, j] = s
            mblk_ref[h, pl.ds(j, 1), :] = jnp.max(s, axis=0, keepdims=True)

    pairwise(i, lambda j: score(j, False))
    score(i, True)

    for h in range(heads):
        bidx = lax.broadcasted_iota(jnp.int32, (n_blocks, blk), 0)
        allowed_max = jnp.where(bidx <= i, mblk_ref[h] + bias_ref[h], -jnp.inf)
        m_ref[h] = jnp.max(allowed_max, axis=0, keepdims=True)
        l_ref[h] = jnp.zeros((1, blk), F32)
        acc_ref[h] = jnp.zeros((hd, blk), F32)

    def attend(j):
        cols = pl.ds(pl.multiple_of(j * blk, blk), blk)
        for h in range(heads):
            p = jnp.exp2(s_ref[h, j] + (bias_ref[h, pl.ds(j, 1), :] - m_ref[h]))
            l_ref[h] += jnp.sum(p, axis=0, keepdims=True)
            acc_ref[h] += _dot(vt_ref[h, :, cols], p.astype(BF16))

    pairwise(i + 1, attend)
    for h, lanes in enumerate(head_lanes):
        o_ref[:, lanes] = (acc_ref[h] / l_ref[h]).T.astype(o_ref.dtype)


def _moba(proj, batch, seq, heads):
    n_t, m, _ = proj.shape
    hd, blk = MOBA_HD, MOBA_BLOCK
    hp = MOBA_HEADS_PER_STEP
    assert seq % blk == 0 and heads * hd == IN_TILE and heads % hp == 0
    n_blocks = seq // blk
    topk = min(MOBA_TOPK, n_blocks - 1)
    wide = hp * hd
    kern = functools.partial(_moba_kernel, n_blocks=n_blocks, topk=topk, heads=hp)
    return pl.pallas_call(
        kern,
        grid=(batch, heads // hp, n_blocks),
        in_specs=[pl.BlockSpec((None, blk, wide), lambda b, g, i: (3, b * n_blocks + i, g)),
                  pl.BlockSpec((None, seq, wide), lambda b, g, i: (4, b, g)),
                  pl.BlockSpec((None, seq, wide), lambda b, g, i: (5, b, g))],
        out_specs=pl.BlockSpec((blk, wide), lambda b, g, i: (b * n_blocks + i, g)),
        out_shape=jax.ShapeDtypeStruct((m, IN_TILE), BF16),
        scratch_shapes=[pltpu.VMEM((hp, n_blocks, hd), F32),
                        pltpu.VMEM((hp, hd, seq), BF16),
                        pltpu.VMEM((hp, n_blocks, blk), F32),
                        pltpu.VMEM((hp, n_blocks, blk), F32),
                        pltpu.VMEM((blk, blk), F32),
                        pltpu.VMEM((hp, n_blocks, blk, blk), F32),
                        pltpu.VMEM((hp, 1, blk), F32),
                        pltpu.VMEM((hp, 1, blk), F32),
                        pltpu.VMEM((hp, hd, blk), F32)],
        compiler_params=_params("arbitrary", "arbitrary", "arbitrary"),
        name="moba",
    )(proj, proj, proj)


def _merge_kernel(og_ref, om_ref, wg_ref, wm_ref, ga_ref, gb_ref, o_ref, wgb_ref, wmb_ref):
    @pl.when(pl.program_id(1) == 0)
    def _():
        wgb_ref[...] = wg_ref[...].astype(BF16)
        wmb_ref[...] = wm_ref[...].astype(BF16)

    ya = _dot(og_ref[...], wgb_ref[...])
    yb = _dot(om_ref[...], wmb_ref[...])
    ga = jax.nn.sigmoid(ga_ref[...].astype(F32))
    gb = jax.nn.sigmoid(gb_ref[...].astype(F32))
    o_ref[...] = (ga * ya + gb * yb).astype(o_ref.dtype)


def _merge(o_gla, o_moba, w_gla_out, w_moba_out, proj, l, d, gate_a0):
    m, kg = o_gla.shape
    km = o_moba.shape[1]
    tm, tn = 1024, IN_TILE
    n_tiles = d // tn
    gate_b0 = gate_a0 + n_tiles
    return pl.pallas_call(
        _merge_kernel,
        grid=(n_tiles, m // tm),
        in_specs=[pl.BlockSpec((tm, kg), lambda j, i: (i, 0)),
                  pl.BlockSpec((tm, km), lambda j, i: (i, 0)),
                  pl.BlockSpec((None, kg, tn), lambda j, i: (l, 0, j)),
                  pl.BlockSpec((None, km, tn), lambda j, i: (l, 0, j)),
                  pl.BlockSpec((None, tm, tn), lambda j, i: (gate_a0 + j, i, 0)),
                  pl.BlockSpec((None, tm, tn), lambda j, i: (gate_b0 + j, i, 0))],
        out_specs=pl.BlockSpec((tm, tn), lambda j, i: (i, j)),
        out_shape=jax.ShapeDtypeStruct((m, d), BF16),
        scratch_shapes=[pltpu.VMEM((kg, tn), BF16), pltpu.VMEM((km, tn), BF16)],
        compiler_params=_params("arbitrary", "arbitrary"),
        name="branch_merge",
    )(o_gla, o_moba, w_gla_out, w_moba_out, proj, proj)


def _matmul_kernel(a_ref, w_ref, o_ref, wbf_ref):
    @pl.when(pl.program_id(1) == 0)
    def _():
        wbf_ref[...] = w_ref[...].astype(BF16)

    o_ref[...] = _dot(a_ref[...], wbf_ref[...]).astype(o_ref.dtype)


def _matmul(a, w, l, tm, tn, name):
    m, k = a.shape
    n = w.shape[-1]
    return pl.pallas_call(
        _matmul_kernel,
        grid=(n // tn, m // tm),
        in_specs=[pl.BlockSpec((tm, k), lambda j, i: (i, 0)),
                  pl.BlockSpec((None, k, tn), lambda j, i: (l, 0, j))],
        out_specs=pl.BlockSpec((tm, tn), lambda j, i: (i, j)),
        out_shape=jax.ShapeDtypeStruct((m, n), F32),
        scratch_shapes=[pltpu.VMEM((k, tn), BF16)],
        compiler_params=_params("arbitrary", "arbitrary"),
        name=name,
    )(a, w)


def _ffn_row_permutation(tile):
    g = tile // V7X_SUBLANES
    pos = np.arange(tile)
    time = (pos % V7X_SUBLANES) * g + pos // V7X_SUBLANES
    perm = np.zeros((tile, tile), np.float32)
    perm[pos, time] = 1.0
    return perm


def _post_mix_kernel(x_ref, y_ref, gpost_ref, mod_ref, gpre_ref, perm_ref, o_ref, h_ref,
                     *, gate_row, shift_row, scale_row):
    x = x_ref[...] + mod_ref[gate_row:gate_row + 1, :] * _rms(y_ref[...], gpost_ref[...])
    o_ref[...] = x
    h = _rms(x, gpre_ref[...])
    h = h * (1.0 + mod_ref[scale_row:scale_row + 1, :]) + mod_ref[shift_row:shift_row + 1, :]
    h_ref[...] = _dot(perm_ref[...], h.astype(BF16)).astype(h_ref.dtype)


def _post_mix(x2, y, gpost3, mod3, gpre3, l, seq):
    m, d = x2.shape
    tm = FFN_TILE
    per_batch = seq // tm
    row_spec = pl.BlockSpec((tm, d), lambda i: (i, 0))
    gain_spec = pl.BlockSpec((None, 1, d), lambda i: (l, 0, 0))
    perm = jnp.asarray(_ffn_row_permutation(tm), BF16)
    return pl.pallas_call(
        functools.partial(_post_mix_kernel, gate_row=2, shift_row=3, scale_row=4),
        grid=(m // tm,),
        in_specs=[row_spec, row_spec, gain_spec,
                  pl.BlockSpec((None, N_MOD, d), lambda i: (i // per_batch, 0, 0)),
                  gain_spec,
                  pl.BlockSpec((tm, tm), lambda i: (0, 0))],
        out_specs=(row_spec, row_spec),
        out_shape=(jax.ShapeDtypeStruct((m, d), F32), jax.ShapeDtypeStruct((m, d), BF16)),
        compiler_params=_params("arbitrary"),
        name="post_mix_pre_ffn",
    )(x2, y, gpost3, mod3, gpre3, perm)


def _post_ffn_kernel(x_ref, y_ref, gpost_ref, mod_ref, unperm_ref, o_ref, *, gate_row):
    y = _dot(unperm_ref[...], jnp.concatenate(_split3(y_ref[...]), axis=0))
    o_ref[...] = x_ref[...] + mod_ref[gate_row:gate_row + 1, :] * _rms(y, gpost_ref[...])


def _post_ffn(x2, y, gpost3, mod3, l, seq):
    m, d = x2.shape
    tm = FFN_TILE
    per_batch = seq // tm
    row_spec = pl.BlockSpec((tm, d), lambda i: (i, 0))
    to_time = _ffn_row_permutation(tm).T
    unperm = jnp.asarray(np.concatenate([to_time, to_time, to_time], axis=1), BF16)
    return pl.pallas_call(
        functools.partial(_post_ffn_kernel, gate_row=5),
        grid=(m // tm,),
        in_specs=[row_spec, row_spec,
                  pl.BlockSpec((None, 1, d), lambda i: (l, 0, 0)),
                  pl.BlockSpec((None, N_MOD, d), lambda i: (i // per_batch, 0, 0)),
                  pl.BlockSpec((tm, 3 * tm), lambda i: (0, 0))],
        out_specs=row_spec,
        out_shape=jax.ShapeDtypeStruct((m, d), F32),
        compiler_params=_params("arbitrary"),
        name="post_ffn",
    )(x2, y, gpost3, mod3, unperm)


def _ffn_up_kernel(h_ref, wu_ref, wg_ref, cu_ref, cg_ref, bu_ref, bg_ref, o_ref,
                   wbf_ref, up_even_ref, up_odd_ref, *, row_tiles, tiles_per_batch, n_pairs):
    s = pl.program_id(0)
    tm, tn = o_ref.shape
    head = 2 * V7X_SUBLANES
    matmul_starts_batch = (s % row_tiles) % tiles_per_batch == 0

    @pl.when(s == 0)
    def _():
        up_even_ref[...] = jnp.zeros_like(up_even_ref)
        up_odd_ref[...] = jnp.zeros_like(up_odd_ref)

    @pl.when(jnp.logical_and(s % row_tiles == 0, s < n_pairs))
    def _():
        wbf_ref[:, :tn] = wu_ref[...].astype(BF16)
        wbf_ref[:, tn:] = wg_ref[...].astype(BF16)

    def step(cur_ref, prev_ref):
        def conv(r0, lo, w_ref, b_ref):
            cols = slice(lo, lo + tn)
            acc = b_ref[...]
            for tap in range(CONV_WIDTH):
                start = pl.multiple_of(r0 + tap * V7X_SUBLANES, V7X_SUBLANES)
                acc = acc + w_ref[tap:tap + 1, :] * prev_ref[pl.ds(start, FFN_ROW_GROUP), cols]
            return acc

        def chunk(c, carry):
            r0 = pl.multiple_of(c * FFN_ROW_CHUNK, FFN_ROW_CHUNK)
            cur_ref[pl.ds(r0 + head, FFN_ROW_CHUNK), :] = _dot(h_ref[pl.ds(r0, FFN_ROW_CHUNK), :],
                                                               wbf_ref[...])
            for r in range(0, FFN_ROW_CHUNK, FFN_ROW_GROUP):
                rg = pl.multiple_of(r0 + r, FFN_ROW_GROUP)
                u = conv(rg, 0, cu_ref, bu_ref)
                g = conv(rg, tn, cg_ref, bg_ref)
                gelu = 0.5 * g * (1.0 + jnp.tanh(np.sqrt(2.0 / np.pi) * (g + 0.044715 * (g * g * g))))
                o_ref[pl.ds(rg, FFN_ROW_GROUP), :] = (gelu * u).astype(o_ref.dtype)
            return carry

        lax.fori_loop(0, tm // FFN_ROW_CHUNK, chunk, 0)

        own = pltpu.roll(cur_ref[tm:tm + head, :], 1, axis=0)
        before = jnp.where(matmul_starts_batch, 0.0,
                           pltpu.roll(prev_ref[tm:tm + head, :], 1 + V7X_SUBLANES, axis=0))
        first = lax.broadcasted_iota(jnp.int32, (head, 1), 0) % V7X_SUBLANES == 0
        cur_ref[0:head, :] = jnp.where(first, before, own)

    @pl.when(s % 2 == 0)
    def _():
        step(up_even_ref, up_odd_ref)

    @pl.when(s % 2 == 1)
    def _():
        step(up_odd_ref, up_even_ref)


def _ffn_up(h, w_up, w_conv, b_conv3, l, seq):
    m, d = h.shape
    f = w_up.shape[-1] // 2
    tm, tn = FFN_TILE, 512
    nt = f // tn
    row_tiles = m // tm
    n_pairs = nt * row_tiles
    kern = functools.partial(_ffn_up_kernel, row_tiles=row_tiles, tiles_per_batch=seq // tm,
                             n_pairs=n_pairs)
    mm = lambda s: jnp.minimum(s, n_pairs - 1)
    ep = lambda s: jnp.maximum(s - 1, 0)
    return pl.pallas_call(
        kern,
        grid=(n_pairs + 1,),
        in_specs=[pl.BlockSpec((tm, d), lambda s: (mm(s) % row_tiles, 0)),
                  pl.BlockSpec((None, d, tn), lambda s: (l, 0, mm(s) // row_tiles)),
                  pl.BlockSpec((None, d, tn), lambda s: (l, 0, nt + mm(s) // row_tiles)),
                  pl.BlockSpec((None, CONV_WIDTH, tn), lambda s: (l, 0, ep(s) // row_tiles)),
                  pl.BlockSpec((None, CONV_WIDTH, tn), lambda s: (l, 0, nt + ep(s) // row_tiles)),
                  pl.BlockSpec((None, 1, tn), lambda s: (l, 0, ep(s) // row_tiles)),
                  pl.BlockSpec((None, 1, tn), lambda s: (l, 0, nt + ep(s) // row_tiles))],
        out_specs=pl.BlockSpec((tm, tn), lambda s: (ep(s) % row_tiles, ep(s) // row_tiles)),
        out_shape=jax.ShapeDtypeStruct((m, f), BF16),
        scratch_shapes=[pltpu.VMEM((d, 2 * tn), BF16),
                        pltpu.VMEM((tm + 2 * V7X_SUBLANES, 2 * tn), F32),
                        pltpu.VMEM((tm + 2 * V7X_SUBLANES, 2 * tn), F32)],
        compiler_params=_params("arbitrary"),
        name="ffn_up_conv_gate",
    )(h, w_up, w_up, w_conv, w_conv, b_conv3, b_conv3)


def kernel(x, c, w_ada, b_ada, g_pre_mix, w_in, w_alpha_up, b_alpha, g_gla_head, w_gla_out,
           w_moba_out, w_o, g_post_mix, g_pre_ffn, w_up, w_conv, b_conv, w_down, g_post_ffn):
    batch, seq, d = x.shape
    depth = w_ada.shape[0]
    m = batch * seq
    gla_k = w_alpha_up.shape[-1]
    gla_heads = gla_k // GLA_DK
    gla_dv = g_gla_head.shape[-1]
    gla_v = w_gla_out.shape[1]
    moba_w = w_moba_out.shape[1]
    moba_heads = moba_w // MOBA_HD
    low_col = 2 * gla_k + 2 * gla_v
    assert gla_v == gla_heads * gla_dv and d % IN_TILE == 0
    assert low_col % IN_TILE == 0 and w_in.shape[-1] == low_col + GLA_RANK + 3 * moba_w + 2 * d
    first_shifted = low_col // IN_TILE
    n_tiles = first_shifted + 3 * moba_w // IN_TILE + 2 * d // IN_TILE

    x2 = x.reshape(m, d)
    c8 = jnp.pad(c, ((0, 8 - batch), (0, 0)))
    as3 = lambda a: a.reshape(a.shape[0], 1, a.shape[-1])
    wup_pad = jnp.pad(w_alpha_up, ((0, 0), (0, V7X_LANES - GLA_RANK), (0, 0)))
    w_in_t = jnp.swapaxes(w_in, 1, 2)

    for l in range(depth):
        mod = _modulation(c8, w_ada, as3(b_ada), l)
        mod3 = mod[:batch].reshape(batch, N_MOD, d)
        h = _prenorm(x2, as3(g_pre_mix), mod3, l, seq, shift_row=0, scale_row=1)
        proj = _in_projection(h, w_in_t, l, n_tiles, first_shifted)
        log_a = _log_alpha(h, w_in_t, wup_pad, as3(b_alpha), l, low_col // V7X_LANES)
        o_gla = _gla(proj, log_a, as3(g_gla_head), l, batch, seq, gla_heads, gla_dv)
        o_moba = _moba(proj, batch, seq, moba_heads)
        merged = _merge(o_gla, o_moba, w_gla_out, w_moba_out, proj, l, d,
                        gate_a0=first_shifted + 3 * moba_w // IN_TILE)
        y = _matmul(merged, w_o, l, 1024, 1024, "out_projection")
        x2, h2 = _post_mix(x2, y, as3(g_post_mix), mod3, as3(g_pre_ffn), l, seq)
        act = _ffn_up(h2, w_up, w_conv, as3(b_conv), l, seq)
        y2 = _matmul(act, w_down, l, 512, 512, "ffn_down")
        x2 = _post_ffn(x2, y2, as3(g_post_ffn), mod3, l, seq)
    return x2.reshape(batch, seq, d)
```

```python
import functools

import numpy as np
import jax
import jax.numpy as jnp
from jax import lax
from jax.experimental import pallas as pl
from jax.experimental.pallas import tpu as pltpu

F32 = jnp.float32
BF16 = jnp.bfloat16

GLA_DK = 128
GLA_RANK = 16
GLA_GATE_TEMP = 16.0
GLA_CHUNK = 64
MOBA_HD = 128
MOBA_BLOCK = 256
MOBA_TOPK = 3
MOBA_HEADS_PER_STEP = 4
CONV_WIDTH = 3
N_MOD = 6
EPS = 1e-6

V7X_LANES = 128
V7X_SUBLANES = 8
V7X_VMEM_LIMIT_BYTES = 56 * 1024 * 1024

IN_TILE = 1024
MASK_VALUE = -1e30
LOG2E = 1.4426950408889634
FFN_TILE = 512
FFN_ROW_CHUNK = 256
FFN_ROW_GROUP = 16


def _params(*semantics):
    return pltpu.CompilerParams(dimension_semantics=semantics,
                                vmem_limit_bytes=V7X_VMEM_LIMIT_BYTES)


def _dot(a, b):
    return jnp.dot(a, b, preferred_element_type=F32)


def _dot_nt(a, b):
    return lax.dot_general(a, b, (((1,), (1,)), ((), ())), preferred_element_type=F32)


def _dot_tn(a, b):
    return lax.dot_general(a, b, (((0,), (0,)), ((), ())), preferred_element_type=F32)


def _split3(x):
    x1 = x.astype(BF16)
    r1 = x - x1.astype(F32)
    x2 = r1.astype(BF16)
    x3 = (r1 - x2.astype(F32)).astype(BF16)
    return x1, x2, x3


def _rms(x, gain):
    return x * lax.rsqrt(jnp.mean(x * x, axis=-1, keepdims=True) + EPS) * gain


def _mod_kernel(c_ref, w_ref, b_ref, o_ref):
    c = c_ref[...]
    s = c * jax.nn.sigmoid(c)
    o_ref[...] = _dot(s.astype(BF16), w_ref[...].astype(BF16)) + b_ref[...]


def _modulation(c8, w_ada, b_ada3, l):
    rows, d = c8.shape
    n = w_ada.shape[-1]
    tn = 1024
    return pl.pallas_call(
        _mod_kernel,
        grid=(n // tn,),
        in_specs=[pl.BlockSpec((rows, d), lambda j: (0, 0)),
                  pl.BlockSpec((None, d, tn), lambda j: (l, 0, j)),
                  pl.BlockSpec((None, 1, tn), lambda j: (l, 0, j))],
        out_specs=pl.BlockSpec((rows, tn), lambda j: (0, j)),
        out_shape=jax.ShapeDtypeStruct((rows, n), F32),
        compiler_params=_params("arbitrary"),
        name="adaln_mod",
    )(c8, w_ada, b_ada3)


def _prenorm_kernel(x_ref, g_ref, mod_ref, w_ref, wup_ref, b_ref, o_ref, la_ref,
                    *, shift_row, scale_row):
    y = _rms(x_ref[...], g_ref[...])
    sh = mod_ref[shift_row:shift_row + 1, :]
    sc = mod_ref[scale_row:scale_row + 1, :]
    h = (y * (1.0 + sc) + sh).astype(o_ref.dtype)
    o_ref[...] = h
    low = _dot_nt(h, w_ref[...].astype(BF16))
    z = _dot(low.astype(BF16), wup_ref[...].astype(BF16)) + b_ref[...]
    log_sig = jnp.minimum(z, 0.0) - jnp.log1p(jnp.exp(-jnp.abs(z)))
    la_ref[...] = log_sig / GLA_GATE_TEMP


def _prenorm(x2, gain3, mod3, w_in_t, wup_pad, b_alpha3, l, seq, shift_row, scale_row, low_block):
    m, d = x2.shape
    k = wup_pad.shape[-1]
    tm = 512
    per_batch = seq // tm
    return pl.pallas_call(
        functools.partial(_prenorm_kernel, shift_row=shift_row, scale_row=scale_row),
        grid=(m // tm,),
        in_specs=[pl.BlockSpec((tm, d), lambda i: (i, 0)),
                  pl.BlockSpec((None, 1, d), lambda i: (l, 0, 0)),
                  pl.BlockSpec((None, N_MOD, d), lambda i: (i // per_batch, 0, 0)),
                  pl.BlockSpec((None, V7X_LANES, d), lambda i: (l, low_block, 0)),
                  pl.BlockSpec((None, V7X_LANES, k), lambda i: (l, 0, 0)),
                  pl.BlockSpec((None, 1, k), lambda i: (l, 0, 0))],
        out_specs=(pl.BlockSpec((tm, d), lambda i: (i, 0)),
                   pl.BlockSpec((tm, k), lambda i: (i, 0))),
        out_shape=(jax.ShapeDtypeStruct((m, d), BF16), jax.ShapeDtypeStruct((m, k), F32)),
        compiler_params=_params("arbitrary"),
        name="prenorm_mix",
    )(x2, gain3, mod3, w_in_t, wup_pad, b_alpha3)


def _inproj_kernel(h_ref, wa_ref, wb_ref, o_ref, wbf_ref, *, first_shifted):
    j = pl.program_id(0)
    i = pl.program_id(1)
    keep = IN_TILE - GLA_RANK

    @pl.when(jnp.logical_and(i == 0, j < first_shifted))
    def _():
        wbf_ref[...] = wa_ref[...].astype(BF16)

    @pl.when(jnp.logical_and(i == 0, j >= first_shifted))
    def _():
        wbf_ref[:keep, :] = wa_ref[GLA_RANK:, :].astype(BF16)
        wbf_ref[keep:, :] = wb_ref[...].astype(BF16)

    o_ref[...] = _dot_nt(h_ref[...], wbf_ref[...]).astype(o_ref.dtype)


def _in_projection(h, w_in_t, l, n_tiles, first_shifted):
    m, d = h.shape
    tm = 1024
    rank_blocks = IN_TILE // GLA_RANK
    return pl.pallas_call(
        functools.partial(_inproj_kernel, first_shifted=first_shifted),
        grid=(n_tiles, m // tm),
        in_specs=[pl.BlockSpec((tm, d), lambda j, i: (i, 0)),
                  pl.BlockSpec((None, IN_TILE, d), lambda j, i: (l, j, 0)),
                  pl.BlockSpec((None, GLA_RANK, d), lambda j, i: (l, (j + 1) * rank_blocks, 0))],
        out_specs=pl.BlockSpec((None, tm, IN_TILE), lambda j, i: (j, i, 0)),
        out_shape=jax.ShapeDtypeStruct((n_tiles, m, IN_TILE), BF16),
        scratch_shapes=[pltpu.VMEM((IN_TILE, d), BF16)],
        compiler_params=_params("arbitrary", "arbitrary"),
        name="in_projection",
    )(h, w_in_t, w_in_t)


def _gla_tables(c):
    levels = int(np.log2(c))
    t = np.arange(c)
    tabs = [(t[None, :] <= t[:, None]),
            (t[None, :] > t[:, None])]
    for lv in range(1, levels + 1):
        size, half = 1 << lv, 1 << (lv - 1)
        mid = (t // size) * size + half
        second = t >= mid
        q_side = second[:, None] & (t[None, :] >= mid[:, None]) & (t[None, :] <= t[:, None])
        k_side = (~second)[:, None] & (t[None, :] > t[:, None]) & (t[None, :] < mid[:, None])
        tabs.append(q_side | k_side)
    e = np.concatenate(tabs, axis=0).astype(np.float32)
    x = t[:, None] ^ t[None, :]
    lvl = np.where(x == 0, 0, np.floor(np.log2(np.maximum(x, 1))).astype(np.int64) + 1)
    lvl = np.where(t[:, None] >= t[None, :], lvl, -1).astype(np.int32)
    return e, lvl, levels


def _gla_kernel(q_ref, k_ref, v_ref, r_ref, la_ref, gh_ref, e_ref, lvl_ref, o_ref, st_ref,
                *, heads, dv, levels, n_chunks):
    c = GLA_CHUNK
    dk = GLA_DK

    @pl.when(pl.program_id(1) == 0)
    def _():
        st_ref[...] = jnp.zeros_like(st_ref)

    e_tab = e_ref[...]
    lvl = lvl_ref[...]
    row = lax.broadcasted_iota(jnp.int32, (c, 1), 0)
    gain = gh_ref[...]

    def chunk(ci, carry):
        rows = pl.ds(pl.multiple_of(ci * c, c), c)
        g3 = jnp.concatenate(_split3(la_ref[rows, :]), axis=0)
        ex = jnp.exp(_dot(e_tab, g3))
        ex_b = ex[0:c]
        ex_last = ex[c:2 * c]
        q_all = q_ref[rows, :].astype(F32) * (dk ** -0.5)
        k_all = k_ref[rows, :].astype(F32)
        for h in range(heads):
            sl = slice(h * dk, (h + 1) * dk)
            vs = slice(h * dv, (h + 1) * dv)
            qh, kh = q_all[:, sl], k_all[:, sl]
            vh = v_ref[rows, vs]
            st = st_ref[h]
            o = _dot_nt((qh * ex_b[:, sl]).astype(BF16), st.astype(BF16))
            a = jnp.where(lvl == 0, _dot_nt(qh.astype(BF16), kh.astype(BF16)), 0.0)
            for lv in range(1, levels + 1):
                e_lv = ex[(1 + lv) * c:(2 + lv) * c, sl]
                second = ((row >> (lv - 1)) & 1) == 1
                ql = jnp.where(second, qh * e_lv, 0.0).astype(BF16)
                kl = jnp.where(second, 0.0, kh * e_lv).astype(BF16)
                a = a + jnp.where(lvl == lv, _dot_nt(ql, kl), 0.0)
            o = o + _dot(a.astype(BF16), vh)
            kt = (kh * ex_last[:, sl]).astype(BF16)
            st_ref[h] = st * ex_b[c - 1:c, sl] + _dot_tn(vh, kt)
            y = _rms(o, gain)
            r = r_ref[rows, vs].astype(F32)
            o_ref[rows, vs] = (y * (r * jax.nn.sigmoid(r))).astype(o_ref.dtype)
        return carry

    lax.fori_loop(0, n_chunks, chunk, 0, unroll=4)


def _gla(proj, log_a, g_head3, l, batch, seq, heads, dv):
    n_t, m, _ = proj.shape
    dk = GLA_DK
    rows = 512
    per_batch = seq // rows
    e_np, lvl_np, levels = _gla_tables(GLA_CHUNK)
    e_np = np.concatenate([e_np, e_np, e_np], axis=1)
    e_tab = jnp.asarray(e_np, BF16)
    lvl = jnp.asarray(lvl_np)
    kw = heads * dk
    assert 2 * kw == IN_TILE and heads * dv == IN_TILE
    kern = functools.partial(_gla_kernel, heads=heads, dv=dv, levels=levels,
                             n_chunks=rows // GLA_CHUNK)
    return pl.pallas_call(
        kern,
        grid=(batch, per_batch),
        in_specs=[pl.BlockSpec((None, rows, kw), lambda b, r: (0, b * per_batch + r, 0)),
                  pl.BlockSpec((None, rows, kw), lambda b, r: (0, b * per_batch + r, 1)),
                  pl.BlockSpec((None, rows, IN_TILE), lambda b, r: (1, b * per_batch + r, 0)),
                  pl.BlockSpec((None, rows, IN_TILE), lambda b, r: (2, b * per_batch + r, 0)),
                  pl.BlockSpec((rows, kw), lambda b, r: (b * per_batch + r, 0)),
                  pl.BlockSpec((None, 1, dv), lambda b, r: (l, 0, 0)),
                  pl.BlockSpec(e_np.shape, lambda b, r: (0, 0)),
                  pl.BlockSpec(lvl_np.shape, lambda b, r: (0, 0))],
        out_specs=pl.BlockSpec((rows, IN_TILE), lambda b, r: (b * per_batch + r, 0)),
        out_shape=jax.ShapeDtypeStruct((m, IN_TILE), BF16),
        scratch_shapes=[pltpu.VMEM((heads, dv, dk), F32)],
        compiler_params=_params("arbitrary", "arbitrary"),
        name="gla",
    )(proj, proj, proj, proj, log_a, g_head3, e_tab, lvl)


def _moba_kernel(q_ref, k_ref, v_ref, o_ref, kmean_ref, vt_ref, bias_ref, mblk_ref, causal_ref,
                 s_ref, m_ref, l_ref, acc_ref, *, n_blocks, topk, heads):
    blk, hd = MOBA_BLOCK, MOBA_HD
    i = pl.program_id(2)
    c2 = (hd ** -0.5) * LOG2E
    head_lanes = [slice(h * hd, (h + 1) * hd) for h in range(heads)]

    @pl.when(i == 0)
    def _():
        for h, lanes in enumerate(head_lanes):
            for n in range(n_blocks):
                rows = slice(n * blk, (n + 1) * blk)
                kmean_ref[h, n:n + 1, :] = jnp.mean(k_ref[rows, lanes].astype(F32), axis=0, keepdims=True)
                vt_ref[h, :, rows] = v_ref[rows, lanes].astype(F32).T.astype(BF16)

        mblk_ref[...] = jnp.zeros_like(mblk_ref)
        kpos = lax.broadcasted_iota(jnp.int32, (blk, blk), 0)
        qpos = lax.broadcasted_iota(jnp.int32, (blk, blk), 1)
        causal_ref[...] = jnp.where(kpos <= qpos, 0.0, MASK_VALUE)

    for h, lanes in enumerate(head_lanes):
        q = q_ref[:, lanes]
        m1, m2, m3 = _split3(kmean_ref[h])
        s_blk = _dot_nt(m1, q) + _dot_nt(m2, q) + _dot_nt(m3, q)
        bidx = lax.broadcasted_iota(jnp.int32, s_blk.shape, 0)
        cur = jnp.where(bidx < i, s_blk, MASK_VALUE)
        sel = jnp.zeros(s_blk.shape, jnp.bool_)
        for _ in range(topk):
            mx = jnp.max(cur, axis=0, keepdims=True)
            idx = jnp.min(jnp.where(cur == mx, bidx, n_blocks), axis=0, keepdims=True)
            pick = bidx == idx
            sel = jnp.logical_or(sel, jnp.logical_and(pick, idx < i))
            cur = jnp.where(pick, -jnp.inf, cur)
        bias_ref[h] = jnp.where(jnp.logical_or(sel, bidx == i), 0.0, MASK_VALUE)

    def pairwise(n, body):
        def two(t, carry):
            body(2 * t)
            body(2 * t + 1)
            return carry

        lax.fori_loop(0, lax.shift_right_logical(n, 1), two, 0)

        @pl.when(n % 2 == 1)
        def _():
            body(n - 1)

    def score(j, causal):
        cols = pl.ds(pl.multiple_of(j * blk, blk), blk)
        for h, lanes in enumerate(head_lanes):
            s = _dot_nt(k_ref[cols, lanes], q_ref[:, lanes]) * c2
            if causal:
                s = s + causal_ref[...]
            s_ref[h, j] = s
            mblk_ref[h, pl.ds(j, 1), :] = jnp.max(s, axis=0, keepdims=True)

    pairwise(i, lambda j: score(j, False))
    score(i, True)

    for h in range(heads):
        bidx = lax.broadcasted_iota(jnp.int32, (n_blocks, blk), 0)
        allowed_max = jnp.where(bidx <= i, mblk_ref[h] + bias_ref[h], -jnp.inf)
        m_ref[h] = jnp.max(allowed_max, axis=0, keepdims=True)
        l_ref[h] = jnp.zeros((1, blk), F32)
        acc_ref[h] = jnp.zeros((hd, blk), F32)

    def attend(j):
        cols = pl.ds(pl.multiple_of(j * blk, blk), blk)
        for h in range(heads):
            p = jnp.exp2(s_ref[h, j] + (bias_ref[h, pl.ds(j, 1), :] - m_ref[h]))
            l_ref[h] += jnp.sum(p, axis=0, keepdims=True)
            acc_ref[h] += _dot(vt_ref[h, :, cols], p.astype(BF16))

    pairwise(i + 1, attend)
    for h, lanes in enumerate(head_lanes):
        o_ref[:, lanes] = (acc_ref[h] / l_ref[h]).T.astype(o_ref.dtype)


def _moba(proj, batch, seq, heads):
    n_t, m, _ = proj.shape
    hd, blk = MOBA_HD, MOBA_BLOCK
    hp = MOBA_HEADS_PER_STEP
    assert seq % blk == 0 and heads * hd == IN_TILE and heads % hp == 0
    n_blocks = seq // blk
    topk = min(MOBA_TOPK, n_blocks - 1)
    wide = hp * hd
    kern = functools.partial(_moba_kernel, n_blocks=n_blocks, topk=topk, heads=hp)
    return pl.pallas_call(
        kern,
        grid=(batch, heads // hp, n_blocks),
        in_specs=[pl.BlockSpec((None, blk, wide), lambda b, g, i: (3, b * n_blocks + i, g)),
                  pl.BlockSpec((None, seq, wide), lambda b, g, i: (4, b, g)),
                  pl.BlockSpec((None, seq, wide), lambda b, g, i: (5, b, g))],
        out_specs=pl.BlockSpec((blk, wide), lambda b, g, i: (b * n_blocks + i, g)),
        out_shape=jax.ShapeDtypeStruct((m, IN_TILE), BF16),
        scratch_shapes=[pltpu.VMEM((hp, n_blocks, hd), F32),
                        pltpu.VMEM((hp, hd, seq), BF16),
                        pltpu.VMEM((hp, n_blocks, blk), F32),
                        pltpu.VMEM((hp, n_blocks, blk), F32),
                        pltpu.VMEM((blk, blk), F32),
                        pltpu.VMEM((hp, n_blocks, blk, blk), F32),
                        pltpu.VMEM((hp, 1, blk), F32),
                        pltpu.VMEM((hp, 1, blk), F32),
                        pltpu.VMEM((hp, hd, blk), F32)],
        compiler_params=_params("arbitrary", "arbitrary", "arbitrary"),
        name="moba",
    )(proj, proj, proj)


def _merge_kernel(og_ref, om_ref, wg_ref, wm_ref, ga_ref, gb_ref, o_ref, wgb_ref, wmb_ref):
    @pl.when(pl.program_id(1) == 0)
    def _():
        wgb_ref[...] = wg_ref[...].astype(BF16)
        wmb_ref[...] = wm_ref[...].astype(BF16)

    ya = _dot(og_ref[...], wgb_ref[...])
    yb = _dot(om_ref[...], wmb_ref[...])
    ga = jax.nn.sigmoid(ga_ref[...].astype(F32))
    gb = jax.nn.sigmoid(gb_ref[...].astype(F32))
    o_ref[...] = (ga * ya + gb * yb).astype(o_ref.dtype)


def _merge(o_gla, o_moba, w_gla_out, w_moba_out, proj, l, d, gate_a0):
    m, kg = o_gla.shape
    km = o_moba.shape[1]
    tm, tn = 1024, IN_TILE
    n_tiles = d // tn
    gate_b0 = gate_a0 + n_tiles
    return pl.pallas_call(
        _merge_kernel,
        grid=(n_tiles, m // tm),
        in_specs=[pl.BlockSpec((tm, kg), lambda j, i: (i, 0)),
                  pl.BlockSpec((tm, km), lambda j, i: (i, 0)),
                  pl.BlockSpec((None, kg, tn), lambda j, i: (l, 0, j)),
                  pl.BlockSpec((None, km, tn), lambda j, i: (l, 0, j)),
                  pl.BlockSpec((None, tm, tn), lambda j, i: (gate_a0 + j, i, 0)),
                  pl.BlockSpec((None, tm, tn), lambda j, i: (gate_b0 + j, i, 0))],
        out_specs=pl.BlockSpec((tm, tn), lambda j, i: (i, j)),
        out_shape=jax.ShapeDtypeStruct((m, d), BF16),
        scratch_shapes=[pltpu.VMEM((kg, tn), BF16), pltpu.VMEM((km, tn), BF16)],
        compiler_params=_params("arbitrary", "arbitrary"),
        name="branch_merge",
    )(o_gla, o_moba, w_gla_out, w_moba_out, proj, proj)


def _matmul_kernel(a_ref, w_ref, o_ref, wbf_ref):
    @pl.when(pl.program_id(1) == 0)
    def _():
        wbf_ref[...] = w_ref[...].astype(BF16)

    o_ref[...] = _dot(a_ref[...], wbf_ref[...]).astype(o_ref.dtype)


def _matmul(a, w, l, tm, tn, name):
    m, k = a.shape
    n = w.shape[-1]
    return pl.pallas_call(
        _matmul_kernel,
        grid=(n // tn, m // tm),
        in_specs=[pl.BlockSpec((tm, k), lambda j, i: (i, 0)),
                  pl.BlockSpec((None, k, tn), lambda j, i: (l, 0, j))],
        out_specs=pl.BlockSpec((tm, tn), lambda j, i: (i, j)),
        out_shape=jax.ShapeDtypeStruct((m, n), BF16),
        scratch_shapes=[pltpu.VMEM((k, tn), BF16)],
        compiler_params=_params("arbitrary", "arbitrary"),
        name=name,
    )(a, w)


def _ffn_row_permutation(tile):
    g = tile // V7X_SUBLANES
    pos = np.arange(tile)
    time = (pos % V7X_SUBLANES) * g + pos // V7X_SUBLANES
    perm = np.zeros((tile, tile), np.float32)
    perm[pos, time] = 1.0
    return perm


def _post_mix_kernel(x_ref, y_ref, gpost_ref, mod_ref, gpre_ref, perm_ref, o_ref, h_ref,
                     *, gate_row, shift_row, scale_row):
    y = y_ref[...].astype(F32)
    x = x_ref[...] + mod_ref[gate_row:gate_row + 1, :] * _rms(y, gpost_ref[...])
    o_ref[...] = x
    h = _rms(x, gpre_ref[...])
    h = h * (1.0 + mod_ref[scale_row:scale_row + 1, :]) + mod_ref[shift_row:shift_row + 1, :]
    h_ref[...] = _dot(perm_ref[...], h.astype(BF16)).astype(h_ref.dtype)


def _post_mix(x2, y, gpost3, mod3, gpre3, l, seq):
    m, d = x2.shape
    tm = FFN_TILE
    per_batch = seq // tm
    row_spec = pl.BlockSpec((tm, d), lambda i: (i, 0))
    gain_spec = pl.BlockSpec((None, 1, d), lambda i: (l, 0, 0))
    perm = jnp.asarray(_ffn_row_permutation(tm), BF16)
    return pl.pallas_call(
        functools.partial(_post_mix_kernel, gate_row=2, shift_row=3, scale_row=4),
        grid=(m // tm,),
        in_specs=[row_spec, row_spec, gain_spec,
                  pl.BlockSpec((None, N_MOD, d), lambda i: (i // per_batch, 0, 0)),
                  gain_spec,
                  pl.BlockSpec((tm, tm), lambda i: (0, 0))],
        out_specs=(row_spec, row_spec),
        out_shape=(jax.ShapeDtypeStruct((m, d), F32), jax.ShapeDtypeStruct((m, d), BF16)),
        compiler_params=_params("arbitrary"),
        name="post_mix_pre_ffn",
    )(x2, y, gpost3, mod3, gpre3, perm)


def _post_ffn_kernel(x_ref, y_ref, gpost_ref, mod_ref, unperm_ref, o_ref, *, gate_row):
    y = _dot(unperm_ref[...], y_ref[...])
    o_ref[...] = x_ref[...] + mod_ref[gate_row:gate_row + 1, :] * _rms(y, gpost_ref[...])


def _post_ffn(x2, y, gpost3, mod3, l, seq):
    m, d = x2.shape
    tm = FFN_TILE
    per_batch = seq // tm
    row_spec = pl.BlockSpec((tm, d), lambda i: (i, 0))
    unperm = jnp.asarray(_ffn_row_permutation(tm).T, BF16)
    return pl.pallas_call(
        functools.partial(_post_ffn_kernel, gate_row=5),
        grid=(m // tm,),
        in_specs=[row_spec, row_spec,
                  pl.BlockSpec((None, 1, d), lambda i: (l, 0, 0)),
                  pl.BlockSpec((None, N_MOD, d), lambda i: (i // per_batch, 0, 0)),
                  pl.BlockSpec((tm, tm), lambda i: (0, 0))],
        out_specs=row_spec,
        out_shape=jax.ShapeDtypeStruct((m, d), F32),
        compiler_params=_params("arbitrary"),
        name="post_ffn",
    )(x2, y, gpost3, mod3, unperm)


def _ffn_up_kernel(h_ref, wu_ref, wg_ref, cu_ref, cg_ref, bu_ref, bg_ref, o_ref,
                   wbf_ref, taps_ref, up_even_ref, up_odd_ref, *, row_tiles, tiles_per_batch,
                   n_pairs):
    s = pl.program_id(0)
    tm, tn = o_ref.shape
    head = 2 * V7X_SUBLANES
    matmul_starts_batch = (s % row_tiles) % tiles_per_batch == 0

    @pl.when(s == 0)
    def _():
        up_even_ref[...] = jnp.zeros_like(up_even_ref)
        up_odd_ref[...] = jnp.zeros_like(up_odd_ref)

    @pl.when(jnp.logical_and(s % row_tiles == 0, s < n_pairs))
    def _():
        wbf_ref[:, :tn] = wu_ref[...].astype(BF16)
        wbf_ref[:, tn:] = wg_ref[...].astype(BF16)

    @pl.when(jnp.maximum(s - 1, 0) % row_tiles == 0)
    def _():
        for half, (w_ref, b_ref) in enumerate(((cu_ref, bu_ref), (cg_ref, bg_ref))):
            for tap in range(CONV_WIDTH):
                taps_ref[half * CONV_WIDTH + tap] = jnp.broadcast_to(w_ref[tap:tap + 1, :],
                                                                     (V7X_SUBLANES, tn))
            taps_ref[2 * CONV_WIDTH + half] = jnp.broadcast_to(b_ref[...], (V7X_SUBLANES, tn))

    def step(cur_ref, prev_ref):
        def rows_of(idx):
            tile = taps_ref[idx]
            return jnp.concatenate([tile] * (FFN_ROW_GROUP // V7X_SUBLANES), axis=0)

        def conv(r0, half):
            cols = slice(half * tn, (half + 1) * tn)
            acc = rows_of(2 * CONV_WIDTH + half)
            for tap in range(CONV_WIDTH):
                start = pl.multiple_of(r0 + tap * V7X_SUBLANES, V7X_SUBLANES)
                acc = acc + rows_of(half * CONV_WIDTH + tap) * prev_ref[pl.ds(start, FFN_ROW_GROUP), cols]
            return acc

        def chunk(c, carry):
            r0 = pl.multiple_of(c * FFN_ROW_CHUNK, FFN_ROW_CHUNK)
            cur_ref[pl.ds(r0 + head, FFN_ROW_CHUNK), :] = _dot(h_ref[pl.ds(r0, FFN_ROW_CHUNK), :],
                                                               wbf_ref[...])
            for r in range(0, FFN_ROW_CHUNK, FFN_ROW_GROUP):
                rg = pl.multiple_of(r0 + r, FFN_ROW_GROUP)
                u = conv(rg, 0)
                g = conv(rg, 1)
                gelu = 0.5 * g * (1.0 + jnp.tanh(np.sqrt(2.0 / np.pi) * (g + 0.044715 * (g * g * g))))
                o_ref[pl.ds(rg, FFN_ROW_GROUP), :] = (gelu * u).astype(o_ref.dtype)
            return carry

        lax.fori_loop(0, tm // FFN_ROW_CHUNK, chunk, 0)

        own = pltpu.roll(cur_ref[tm:tm + head, :], 1, axis=0)
        before = jnp.where(matmul_starts_batch, 0.0,
                           pltpu.roll(prev_ref[tm:tm + head, :], 1 + V7X_SUBLANES, axis=0))
        first = lax.broadcasted_iota(jnp.int32, (head, 1), 0) % V7X_SUBLANES == 0
        cur_ref[0:head, :] = jnp.where(first, before, own)

    @pl.when(s % 2 == 0)
    def _():
        step(up_even_ref, up_odd_ref)

    @pl.when(s % 2 == 1)
    def _():
        step(up_odd_ref, up_even_ref)


def _ffn_up(h, w_up, w_conv, b_conv3, l, seq):
    m, d = h.shape
    f = w_up.shape[-1] // 2
    tm, tn = FFN_TILE, 512
    nt = f // tn
    row_tiles = m // tm
    n_pairs = nt * row_tiles
    kern = functools.partial(_ffn_up_kernel, row_tiles=row_tiles, tiles_per_batch=seq // tm,
                             n_pairs=n_pairs)
    mm = lambda s: jnp.minimum(s, n_pairs - 1)
    ep = lambda s: jnp.maximum(s - 1, 0)
    return pl.pallas_call(
        kern,
        grid=(n_pairs + 1,),
        in_specs=[pl.BlockSpec((tm, d), lambda s: (mm(s) % row_tiles, 0)),
                  pl.BlockSpec((None, d, tn), lambda s: (l, 0, mm(s) // row_tiles)),
                  pl.BlockSpec((None, d, tn), lambda s: (l, 0, nt + mm(s) // row_tiles)),
                  pl.BlockSpec((None, CONV_WIDTH, tn), lambda s: (l, 0, ep(s) // row_tiles)),
                  pl.BlockSpec((None, CONV_WIDTH, tn), lambda s: (l, 0, nt + ep(s) // row_tiles)),
                  pl.BlockSpec((None, 1, tn), lambda s: (l, 0, ep(s) // row_tiles)),
                  pl.BlockSpec((None, 1, tn), lambda s: (l, 0, nt + ep(s) // row_tiles))],
        out_specs=pl.BlockSpec((tm, tn), lambda s: (ep(s) % row_tiles, ep(s) // row_tiles)),
        out_shape=jax.ShapeDtypeStruct((m, f), BF16),
        scratch_shapes=[pltpu.VMEM((d, 2 * tn), BF16),
                        pltpu.VMEM((2 * CONV_WIDTH + 2, V7X_SUBLANES, tn), F32),
                        pltpu.VMEM((tm + 2 * V7X_SUBLANES, 2 * tn), F32),
                        pltpu.VMEM((tm + 2 * V7X_SUBLANES, 2 * tn), F32)],
        compiler_params=_params("arbitrary"),
        name="ffn_up_conv_gate",
    )(h, w_up, w_up, w_conv, w_conv, b_conv3, b_conv3)


def kernel(x, c, w_ada, b_ada, g_pre_mix, w_in, w_alpha_up, b_alpha, g_gla_head, w_gla_out,
           w_moba_out, w_o, g_post_mix, g_pre_ffn, w_up, w_conv, b_conv, w_down, g_post_ffn):
    batch, seq, d = x.shape
    depth = w_ada.shape[0]
    m = batch * seq
    gla_k = w_alpha_up.shape[-1]
    gla_heads = gla_k // GLA_DK
    gla_dv = g_gla_head.shape[-1]
    gla_v = w_gla_out.shape[1]
    moba_w = w_moba_out.shape[1]
    moba_heads = moba_w // MOBA_HD
    low_col = 2 * gla_k + 2 * gla_v
    assert gla_v == gla_heads * gla_dv and d % IN_TILE == 0
    assert low_col % IN_TILE == 0 and w_in.shape[-1] == low_col + GLA_RANK + 3 * moba_w + 2 * d
    first_shifted = low_col // IN_TILE
    n_tiles = first_shifted + 3 * moba_w // IN_TILE + 2 * d // IN_TILE

    x2 = x.reshape(m, d)
    c8 = jnp.pad(c, ((0, 8 - batch), (0, 0)))
    as3 = lambda a: a.reshape(a.shape[0], 1, a.shape[-1])
    wup_pad = jnp.pad(w_alpha_up, ((0, 0), (0, V7X_LANES - GLA_RANK), (0, 0)))
    w_in_t = jnp.swapaxes(w_in, 1, 2)

    for l in range(depth):
        mod = _modulation(c8, w_ada, as3(b_ada), l)
        mod3 = mod[:batch].reshape(batch, N_MOD, d)
        h, log_a = _prenorm(x2, as3(g_pre_mix), mod3, w_in_t, wup_pad, as3(b_alpha), l, seq,
                            shift_row=0, scale_row=1, low_block=low_col // V7X_LANES)
        proj = _in_projection(h, w_in_t, l, n_tiles, first_shifted)
        o_gla = _gla(proj, log_a, as3(g_gla_head), l, batch, seq, gla_heads, gla_dv)
        o_moba = _moba(proj, batch, seq, moba_heads)
        merged = _merge(o_gla, o_moba, w_gla_out, w_moba_out, proj, l, d,
                        gate_a0=first_shifted + 3 * moba_w // IN_TILE)
        y = _matmul(merged, w_o, l, 1024, 1024, "out_projection")
        x2, h2 = _post_mix(x2, y, as3(g_post_mix), mod3, as3(g_pre_ffn), l, seq)
        act = _ffn_up(h2, w_up, w_conv, as3(b_conv), l, seq)
        y2 = _matmul(act, w_down, l, 512, 512, "ffn_down")
        x2 = _post_ffn(x2, y2, as3(g_post_ffn), mod3, l, seq)
    return x2.reshape(batch, seq, d)
```

```python
import functools

import numpy as np
import jax
import jax.numpy as jnp
from jax import lax
from jax.experimental import pallas as pl
from jax.experimental.pallas import tpu as pltpu

F32 = jnp.float32
BF16 = jnp.bfloat16

GLA_DK = 128
GLA_RANK = 16
GLA_GATE_TEMP = 16.0
GLA_CHUNK = 64
MOBA_HD = 128
MOBA_BLOCK = 256
MOBA_TOPK = 3
MOBA_HEADS_PER_STEP = 2
CONV_WIDTH = 3
N_MOD = 6
EPS = 1e-6

V7X_LANES = 128
V7X_SUBLANES = 8
V7X_VMEM_LIMIT_BYTES = 56 * 1024 * 1024

IN_TILE = 1024
MASK_VALUE = -1e30
LOG2E = 1.4426950408889634
FFN_TILE = 512
FFN_ROW_CHUNK = 256


def _params(*semantics):
    return pltpu.CompilerParams(dimension_semantics=semantics,
                                vmem_limit_bytes=V7X_VMEM_LIMIT_BYTES)


def _dot(a, b):
    return jnp.dot(a, b, preferred_element_type=F32)


def _dot_nt(a, b):
    return lax.dot_general(a, b, (((1,), (1,)), ((), ())), preferred_element_type=F32)


def _dot_tn(a, b):
    return lax.dot_general(a, b, (((0,), (0,)), ((), ())), preferred_element_type=F32)


def _split3(x):
    x1 = x.astype(BF16)
    r1 = x - x1.astype(F32)
    x2 = r1.astype(BF16)
    x3 = (r1 - x2.astype(F32)).astype(BF16)
    return x1, x2, x3


def _rms(x, gain):
    return x * lax.rsqrt(jnp.mean(x * x, axis=-1, keepdims=True) + EPS) * gain


def _mod_kernel(c_ref, w_ref, b_ref, o_ref):
    c = c_ref[...]
    s = c * jax.nn.sigmoid(c)
    o_ref[...] = _dot(s.astype(BF16), w_ref[...].astype(BF16)) + b_ref[...]


def _modulation(c8, w_ada, b_ada3, l):
    rows, d = c8.shape
    n = w_ada.shape[-1]
    tn = 1024
    return pl.pallas_call(
        _mod_kernel,
        grid=(n // tn,),
        in_specs=[pl.BlockSpec((rows, d), lambda j: (0, 0)),
                  pl.BlockSpec((None, d, tn), lambda j: (l, 0, j)),
                  pl.BlockSpec((None, 1, tn), lambda j: (l, 0, j))],
        out_specs=pl.BlockSpec((rows, tn), lambda j: (0, j)),
        out_shape=jax.ShapeDtypeStruct((rows, n), F32),
        compiler_params=_params("arbitrary"),
        name="adaln_mod",
    )(c8, w_ada, b_ada3)


def _prenorm_kernel(x_ref, g_ref, mod_ref, w_ref, wup_ref, b_ref, o_ref, la_ref,
                    *, shift_row, scale_row):
    y = _rms(x_ref[...], g_ref[...])
    sh = mod_ref[shift_row:shift_row + 1, :]
    sc = mod_ref[scale_row:scale_row + 1, :]
    h = (y * (1.0 + sc) + sh).astype(o_ref.dtype)
    o_ref[...] = h
    low = _dot_nt(h, w_ref[...].astype(BF16))
    z = _dot(low.astype(BF16), wup_ref[...].astype(BF16)) + b_ref[...]
    log_sig = jnp.minimum(z, 0.0) - jnp.log1p(jnp.exp(-jnp.abs(z)))
    la_ref[...] = log_sig / GLA_GATE_TEMP


def _prenorm(x2, gain3, mod3, w_in_t, wup_pad, b_alpha3, l, seq, shift_row, scale_row, low_block):
    m, d = x2.shape
    k = wup_pad.shape[-1]
    tm = 512
    per_batch = seq // tm
    return pl.pallas_call(
        functools.partial(_prenorm_kernel, shift_row=shift_row, scale_row=scale_row),
        grid=(m // tm,),
        in_specs=[pl.BlockSpec((tm, d), lambda i: (i, 0)),
                  pl.BlockSpec((None, 1, d), lambda i: (l, 0, 0)),
                  pl.BlockSpec((None, N_MOD, d), lambda i: (i // per_batch, 0, 0)),
                  pl.BlockSpec((None, V7X_LANES, d), lambda i: (l, low_block, 0)),
                  pl.BlockSpec((None, V7X_LANES, k), lambda i: (l, 0, 0)),
                  pl.BlockSpec((None, 1, k), lambda i: (l, 0, 0))],
        out_specs=(pl.BlockSpec((tm, d), lambda i: (i, 0)),
                   pl.BlockSpec((tm, k), lambda i: (i, 0))),
        out_shape=(jax.ShapeDtypeStruct((m, d), BF16), jax.ShapeDtypeStruct((m, k), F32)),
        compiler_params=_params("arbitrary"),
        name="prenorm_mix",
    )(x2, gain3, mod3, w_in_t, wup_pad, b_alpha3)


def _inproj_kernel(h_ref, wa_ref, wb_ref, o_ref, wbf_ref, *, first_shifted):
    j = pl.program_id(0)
    i = pl.program_id(1)
    keep = IN_TILE - GLA_RANK

    @pl.when(jnp.logical_and(i == 0, j < first_shifted))
    def _():
        wbf_ref[...] = wa_ref[...].astype(BF16)

    @pl.when(jnp.logical_and(i == 0, j >= first_shifted))
    def _():
        wbf_ref[:keep, :] = wa_ref[GLA_RANK:, :].astype(BF16)
        wbf_ref[keep:, :] = wb_ref[...].astype(BF16)

    o_ref[...] = _dot_nt(h_ref[...], wbf_ref[...]).astype(o_ref.dtype)


def _in_projection(h, w_in_t, l, n_tiles, first_shifted):
    m, d = h.shape
    tm = 1024
    rank_blocks = IN_TILE // GLA_RANK
    return pl.pallas_call(
        functools.partial(_inproj_kernel, first_shifted=first_shifted),
        grid=(n_tiles, m // tm),
        in_specs=[pl.BlockSpec((tm, d), lambda j, i: (i, 0)),
                  pl.BlockSpec((None, IN_TILE, d), lambda j, i: (l, j, 0)),
                  pl.BlockSpec((None, GLA_RANK, d), lambda j, i: (l, (j + 1) * rank_blocks, 0))],
        out_specs=pl.BlockSpec((None, tm, IN_TILE), lambda j, i: (j, i, 0)),
        out_shape=jax.ShapeDtypeStruct((n_tiles, m, IN_TILE), BF16),
        scratch_shapes=[pltpu.VMEM((IN_TILE, d), BF16)],
        compiler_params=_params("arbitrary", "arbitrary"),
        name="in_projection",
    )(h, w_in_t, w_in_t)


def _gla_tables(c):
    levels = int(np.log2(c))
    t = np.arange(c)
    tabs = [(t[None, :] <= t[:, None]),
            (t[None, :] > t[:, None])]
    for lv in range(1, levels + 1):
        size, half = 1 << lv, 1 << (lv - 1)
        mid = (t // size) * size + half
        second = t >= mid
        q_side = second[:, None] & (t[None, :] >= mid[:, None]) & (t[None, :] <= t[:, None])
        k_side = (~second)[:, None] & (t[None, :] > t[:, None]) & (t[None, :] < mid[:, None])
        tabs.append(q_side | k_side)
    e = np.concatenate(tabs, axis=0).astype(np.float32)
    x = t[:, None] ^ t[None, :]
    lvl = np.where(x == 0, 0, np.floor(np.log2(np.maximum(x, 1))).astype(np.int64) + 1)
    lvl = np.where(t[:, None] >= t[None, :], lvl, -1).astype(np.int32)
    return e, lvl, levels


def _gla_kernel(q_ref, k_ref, v_ref, r_ref, la_ref, gh_ref, e_ref, lvl_ref, o_ref, st_ref,
                *, heads, dv, levels, n_chunks):
    c = GLA_CHUNK
    dk = GLA_DK

    @pl.when(pl.program_id(1) == 0)
    def _():
        st_ref[...] = jnp.zeros_like(st_ref)

    e_tab = e_ref[...]
    lvl = lvl_ref[...]
    row = lax.broadcasted_iota(jnp.int32, (c, 1), 0)
    gain = gh_ref[...]

    def chunk(ci, carry):
        rows = pl.ds(pl.multiple_of(ci * c, c), c)
        g3 = jnp.concatenate(_split3(la_ref[rows, :]), axis=0)
        ex = jnp.exp(_dot(e_tab, g3))
        ex_b = ex[0:c]
        ex_last = ex[c:2 * c]
        q_all = q_ref[rows, :].astype(F32) * (dk ** -0.5)
        k_all = k_ref[rows, :].astype(F32)
        for h in range(heads):
            sl = slice(h * dk, (h + 1) * dk)
            vs = slice(h * dv, (h + 1) * dv)
            qh, kh = q_all[:, sl], k_all[:, sl]
            vh = v_ref[rows, vs]
            st = st_ref[h]
            o = _dot_nt((qh * ex_b[:, sl]).astype(BF16), st.astype(BF16))
            a = jnp.where(lvl == 0, _dot_nt(qh.astype(BF16), kh.astype(BF16)), 0.0)
            for lv in range(1, levels + 1):
                e_lv = ex[(1 + lv) * c:(2 + lv) * c, sl]
                second = ((row >> (lv - 1)) & 1) == 1
                ql = jnp.where(second, qh * e_lv, 0.0).astype(BF16)
                kl = jnp.where(second, 0.0, kh * e_lv).astype(BF16)
                a = a + jnp.where(lvl == lv, _dot_nt(ql, kl), 0.0)
            o = o + _dot(a.astype(BF16), vh)
            kt = (kh * ex_last[:, sl]).astype(BF16)
            st_ref[h] = st * ex_b[c - 1:c, sl] + _dot_tn(vh, kt)
            y = _rms(o, gain)
            r = r_ref[rows, vs].astype(F32)
            o_ref[rows, vs] = (y * (r * jax.nn.sigmoid(r))).astype(o_ref.dtype)
        return carry

    lax.fori_loop(0, n_chunks, chunk, 0, unroll=4)


def _gla(proj, log_a, g_head3, l, batch, seq, heads, dv):
    n_t, m, _ = proj.shape
    dk = GLA_DK
    rows = 512
    per_batch = seq // rows
    e_np, lvl_np, levels = _gla_tables(GLA_CHUNK)
    e_np = np.concatenate([e_np, e_np, e_np], axis=1)
    e_tab = jnp.asarray(e_np, BF16)
    lvl = jnp.asarray(lvl_np)
    kw = heads * dk
    assert 2 * kw == IN_TILE and heads * dv == IN_TILE
    kern = functools.partial(_gla_kernel, heads=heads, dv=dv, levels=levels,
                             n_chunks=rows // GLA_CHUNK)
    return pl.pallas_call(
        kern,
        grid=(batch, per_batch),
        in_specs=[pl.BlockSpec((None, rows, kw), lambda b, r: (0, b * per_batch + r, 0)),
                  pl.BlockSpec((None, rows, kw), lambda b, r: (0, b * per_batch + r, 1)),
                  pl.BlockSpec((None, rows, IN_TILE), lambda b, r: (1, b * per_batch + r, 0)),
                  pl.BlockSpec((None, rows, IN_TILE), lambda b, r: (2, b * per_batch + r, 0)),
                  pl.BlockSpec((rows, kw), lambda b, r: (b * per_batch + r, 0)),
                  pl.BlockSpec((None, 1, dv), lambda b, r: (l, 0, 0)),
                  pl.BlockSpec(e_np.shape, lambda b, r: (0, 0)),
                  pl.BlockSpec(lvl_np.shape, lambda b, r: (0, 0))],
        out_specs=pl.BlockSpec((rows, IN_TILE), lambda b, r: (b * per_batch + r, 0)),
        out_shape=jax.ShapeDtypeStruct((m, IN_TILE), BF16),
        scratch_shapes=[pltpu.VMEM((heads, dv, dk), F32)],
        compiler_params=_params("arbitrary", "arbitrary"),
        name="gla",
    )(proj, proj, proj, proj, log_a, g_head3, e_tab, lvl)


def _moba_kernel(q_ref, k_ref, v_ref, o_ref, kmean_ref, vt_ref, causal_ref, *tile_state,
                 n_blocks, topk, heads):
    blk, hd = MOBA_BLOCK, MOBA_HD
    i = pl.program_id(2)
    c2 = (hd ** -0.5) * LOG2E
    head_lanes = [slice(h * hd, (h + 1) * hd) for h in range(heads)]
    per_tile = len(tile_state) // 2
    even, odd = tile_state[:per_tile], tile_state[per_tile:]
    has_tile = i < n_blocks

    @pl.when(i == 0)
    def _():
        for h, lanes in enumerate(head_lanes):
            for n in range(n_blocks):
                rows = slice(n * blk, (n + 1) * blk)
                kmean_ref[h, n:n + 1, :] = jnp.mean(k_ref[rows, lanes].astype(F32), axis=0, keepdims=True)
                vt_ref[h, :, rows] = v_ref[rows, lanes].astype(F32).T.astype(BF16)

        kpos = lax.broadcasted_iota(jnp.int32, (blk, blk), 0)
        qpos = lax.broadcasted_iota(jnp.int32, (blk, blk), 1)
        causal_ref[...] = jnp.where(kpos <= qpos, 0.0, MASK_VALUE)
        for _, mblk_ref, _, _, l_ref, acc_ref in (even, odd):
            mblk_ref[...] = jnp.zeros_like(mblk_ref)
            l_ref[...] = jnp.ones_like(l_ref)
            acc_ref[...] = jnp.zeros_like(acc_ref)

    def pairwise(n, body):
        def two(t, carry):
            body(2 * t)
            body(2 * t + 1)
            return carry

        lax.fori_loop(0, lax.shift_right_logical(n, 1), two, 0)

        @pl.when(n % 2 == 1)
        def _():
            body(n - 1)

    def step(this, last):
        bias_ref, mblk_ref, s_ref, m_ref, l_ref, acc_ref = this
        bias_last, _, s_last, m_last, l_last, acc_last = last

        @pl.when(has_tile)
        def _():
            for h, lanes in enumerate(head_lanes):
                q = q_ref[:, lanes]
                m1, m2, m3 = _split3(kmean_ref[h])
                s_blk = _dot_nt(m1, q) + _dot_nt(m2, q) + _dot_nt(m3, q)
                bidx = lax.broadcasted_iota(jnp.int32, s_blk.shape, 0)
                cur = jnp.where(bidx < i, s_blk, MASK_VALUE)
                sel = jnp.zeros(s_blk.shape, jnp.bool_)
                for _ in range(topk):
                    mx = jnp.max(cur, axis=0, keepdims=True)
                    idx = jnp.min(jnp.where(cur == mx, bidx, n_blocks), axis=0, keepdims=True)
                    pick = bidx == idx
                    sel = jnp.logical_or(sel, jnp.logical_and(pick, idx < i))
                    cur = jnp.where(pick, -jnp.inf, cur)
                bias_ref[h] = jnp.where(jnp.logical_or(sel, bidx == i), 0.0, MASK_VALUE)

        def score(j, causal):
            cols = pl.ds(pl.multiple_of(j * blk, blk), blk)
            for h, lanes in enumerate(head_lanes):
                s = _dot_nt(k_ref[cols, lanes], q_ref[:, lanes]) * c2
                if causal:
                    s = s + causal_ref[...]
                s_ref[h, j] = s
                mblk_ref[h, pl.ds(j, 1), :] = jnp.max(s, axis=0, keepdims=True)

        def attend(j):
            cols = pl.ds(pl.multiple_of(j * blk, blk), blk)
            for h in range(heads):
                p = jnp.exp2(s_last[h, j] + (bias_last[h, pl.ds(j, 1), :] - m_last[h]))
                l_last[h] += jnp.sum(p, axis=0, keepdims=True)
                acc_last[h] += _dot(vt_ref[h, :, cols], p.astype(BF16))

        def both(j):
            score(j, False)
            attend(j)

        @pl.when(has_tile)
        def _():
            pairwise(i, both)
            score(i, True)
            for h in range(heads):
                bidx = lax.broadcasted_iota(jnp.int32, (n_blocks, blk), 0)
                allowed_max = jnp.where(bidx <= i, mblk_ref[h] + bias_ref[h], -jnp.inf)
                m_ref[h] = jnp.max(allowed_max, axis=0, keepdims=True)
                l_ref[h] = jnp.zeros((1, blk), F32)
                acc_ref[h] = jnp.zeros((hd, blk), F32)

        @pl.when(jnp.logical_not(has_tile))
        def _():
            pairwise(i, attend)

        for h, lanes in enumerate(head_lanes):
            o_ref[:, lanes] = (acc_last[h] / l_last[h]).T.astype(o_ref.dtype)

    @pl.when(i % 2 == 0)
    def _():
        step(even, odd)

    @pl.when(i % 2 == 1)
    def _():
        step(odd, even)


def _moba(proj, batch, seq, heads):
    n_t, m, _ = proj.shape
    hd, blk = MOBA_HD, MOBA_BLOCK
    hp = MOBA_HEADS_PER_STEP
    assert seq % blk == 0 and heads * hd == IN_TILE and heads % hp == 0
    n_blocks = seq // blk
    topk = min(MOBA_TOPK, n_blocks - 1)
    wide = hp * hd
    kern = functools.partial(_moba_kernel, n_blocks=n_blocks, topk=topk, heads=hp)
    tile_state = [pltpu.VMEM((hp, n_blocks, blk), F32),
                  pltpu.VMEM((hp, n_blocks, blk), F32),
                  pltpu.VMEM((hp, n_blocks, blk, blk), F32),
                  pltpu.VMEM((hp, 1, blk), F32),
                  pltpu.VMEM((hp, 1, blk), F32),
                  pltpu.VMEM((hp, hd, blk), F32)]
    q_tile = lambda i: jnp.minimum(i, n_blocks - 1)
    o_tile = lambda i: jnp.maximum(i - 1, 0)
    return pl.pallas_call(
        kern,
        grid=(batch, heads // hp, n_blocks + 1),
        in_specs=[pl.BlockSpec((None, blk, wide), lambda b, g, i: (3, b * n_blocks + q_tile(i), g)),
                  pl.BlockSpec((None, seq, wide), lambda b, g, i: (4, b, g)),
                  pl.BlockSpec((None, seq, wide), lambda b, g, i: (5, b, g))],
        out_specs=pl.BlockSpec((blk, wide), lambda b, g, i: (b * n_blocks + o_tile(i), g)),
        out_shape=jax.ShapeDtypeStruct((m, IN_TILE), BF16),
        scratch_shapes=[pltpu.VMEM((hp, n_blocks, hd), F32),
                        pltpu.VMEM((hp, hd, seq), BF16),
                        pltpu.VMEM((blk, blk), F32)] + tile_state + tile_state,
        compiler_params=_params("arbitrary", "arbitrary", "arbitrary"),
        name="moba",
    )(proj, proj, proj)


def _merge_kernel(og_ref, om_ref, wg_ref, wm_ref, ga_ref, gb_ref, o_ref, wgb_ref, wmb_ref):
    @pl.when(pl.program_id(1) == 0)
    def _():
        wgb_ref[...] = wg_ref[...].astype(BF16)
        wmb_ref[...] = wm_ref[...].astype(BF16)

    ya = _dot(og_ref[...], wgb_ref[...])
    yb = _dot(om_ref[...], wmb_ref[...])
    ga = jax.nn.sigmoid(ga_ref[...].astype(F32))
    gb = jax.nn.sigmoid(gb_ref[...].astype(F32))
    o_ref[...] = (ga * ya + gb * yb).astype(o_ref.dtype)


def _merge(o_gla, o_moba, w_gla_out, w_moba_out, proj, l, d, gate_a0):
    m, kg = o_gla.shape
    km = o_moba.shape[1]
    tm, tn = 1024, IN_TILE
    n_tiles = d // tn
    gate_b0 = gate_a0 + n_tiles
    return pl.pallas_call(
        _merge_kernel,
        grid=(n_tiles, m // tm),
        in_specs=[pl.BlockSpec((tm, kg), lambda j, i: (i, 0)),
                  pl.BlockSpec((tm, km), lambda j, i: (i, 0)),
                  pl.BlockSpec((None, kg, tn), lambda j, i: (l, 0, j)),
                  pl.BlockSpec((None, km, tn), lambda j, i: (l, 0, j)),
                  pl.BlockSpec((None, tm, tn), lambda j, i: (gate_a0 + j, i, 0)),
                  pl.BlockSpec((None, tm, tn), lambda j, i: (gate_b0 + j, i, 0))],
        out_specs=pl.BlockSpec((tm, tn), lambda j, i: (i, j)),
        out_shape=jax.ShapeDtypeStruct((m, d), BF16),
        scratch_shapes=[pltpu.VMEM((kg, tn), BF16), pltpu.VMEM((km, tn), BF16)],
        compiler_params=_params("arbitrary", "arbitrary"),
        name="branch_merge",
    )(o_gla, o_moba, w_gla_out, w_moba_out, proj, proj)


def _matmul_kernel(a_ref, w_ref, o_ref, wbf_ref):
    @pl.when(pl.program_id(1) == 0)
    def _():
        wbf_ref[...] = w_ref[...].astype(BF16)

    o_ref[...] = _dot(a_ref[...], wbf_ref[...]).astype(o_ref.dtype)


def _matmul(a, w, l, tm, tn, name):
    m, k = a.shape
    n = w.shape[-1]
    return pl.pallas_call(
        _matmul_kernel,
        grid=(n // tn, m // tm),
        in_specs=[pl.BlockSpec((tm, k), lambda j, i: (i, 0)),
                  pl.BlockSpec((None, k, tn), lambda j, i: (l, 0, j))],
        out_specs=pl.BlockSpec((tm, tn), lambda j, i: (i, j)),
        out_shape=jax.ShapeDtypeStruct((m, n), BF16),
        scratch_shapes=[pltpu.VMEM((k, tn), BF16)],
        compiler_params=_params("arbitrary", "arbitrary"),
        name=name,
    )(a, w)


def _ffn_row_permutation(tile):
    g = tile // V7X_SUBLANES
    pos = np.arange(tile)
    time = (pos % V7X_SUBLANES) * g + pos // V7X_SUBLANES
    perm = np.zeros((tile, tile), np.float32)
    perm[pos, time] = 1.0
    return perm


def _post_mix_kernel(x_ref, y_ref, gpost_ref, mod_ref, gpre_ref, perm_ref, o_ref, h_ref,
                     *, gate_row, shift_row, scale_row):
    y = y_ref[...].astype(F32)
    x = x_ref[...] + mod_ref[gate_row:gate_row + 1, :] * _rms(y, gpost_ref[...])
    o_ref[...] = x
    h = _rms(x, gpre_ref[...])
    h = h * (1.0 + mod_ref[scale_row:scale_row + 1, :]) + mod_ref[shift_row:shift_row + 1, :]
    h_ref[...] = _dot(perm_ref[...], h.astype(BF16)).astype(h_ref.dtype)


def _post_mix(x2, y, gpost3, mod3, gpre3, l, seq):
    m, d = x2.shape
    tm = FFN_TILE
    per_batch = seq // tm
    row_spec = pl.BlockSpec((tm, d), lambda i: (i, 0))
    gain_spec = pl.BlockSpec((None, 1, d), lambda i: (l, 0, 0))
    perm = jnp.asarray(_ffn_row_permutation(tm), BF16)
    return pl.pallas_call(
        functools.partial(_post_mix_kernel, gate_row=2, shift_row=3, scale_row=4),
        grid=(m // tm,),
        in_specs=[row_spec, row_spec, gain_spec,
                  pl.BlockSpec((None, N_MOD, d), lambda i: (i // per_batch, 0, 0)),
                  gain_spec,
                  pl.BlockSpec((tm, tm), lambda i: (0, 0))],
        out_specs=(row_spec, row_spec),
        out_shape=(jax.ShapeDtypeStruct((m, d), F32), jax.ShapeDtypeStruct((m, d), BF16)),
        compiler_params=_params("arbitrary"),
        name="post_mix_pre_ffn",
    )(x2, y, gpost3, mod3, gpre3, perm)


def _post_ffn_kernel(x_ref, y_ref, gpost_ref, mod_ref, unperm_ref, o_ref, *, gate_row):
    y = _dot(unperm_ref[...], y_ref[...])
    o_ref[...] = x_ref[...] + mod_ref[gate_row:gate_row + 1, :] * _rms(y, gpost_ref[...])


def _post_ffn(x2, y, gpost3, mod3, l, seq):
    m, d = x2.shape
    tm = FFN_TILE
    per_batch = seq // tm
    row_spec = pl.BlockSpec((tm, d), lambda i: (i, 0))
    unperm = jnp.asarray(_ffn_row_permutation(tm).T, BF16)
    return pl.pallas_call(
        functools.partial(_post_ffn_kernel, gate_row=5),
        grid=(m // tm,),
        in_specs=[row_spec, row_spec,
                  pl.BlockSpec((None, 1, d), lambda i: (l, 0, 0)),
                  pl.BlockSpec((None, N_MOD, d), lambda i: (i // per_batch, 0, 0)),
                  pl.BlockSpec((tm, tm), lambda i: (0, 0))],
        out_specs=row_spec,
        out_shape=jax.ShapeDtypeStruct((m, d), F32),
        compiler_params=_params("arbitrary"),
        name="post_ffn",
    )(x2, y, gpost3, mod3, unperm)


def _ffn_up_kernel(h_ref, wu_ref, wg_ref, cu_ref, cg_ref, bu_ref, bg_ref, o_ref,
                   wbf_ref, up_even_ref, up_odd_ref, *, row_tiles, tiles_per_batch, n_pairs):
    s = pl.program_id(0)
    tm, tn = o_ref.shape
    head = 2 * V7X_SUBLANES
    matmul_starts_batch = (s % row_tiles) % tiles_per_batch == 0

    @pl.when(s == 0)
    def _():
        up_even_ref[...] = jnp.zeros_like(up_even_ref)
        up_odd_ref[...] = jnp.zeros_like(up_odd_ref)

    @pl.when(jnp.logical_and(s % row_tiles == 0, s < n_pairs))
    def _():
        wbf_ref[:, :tn] = wu_ref[...].astype(BF16)
        wbf_ref[:, tn:] = wg_ref[...].astype(BF16)

    def step(cur_ref, prev_ref):
        sub = V7X_SUBLANES

        def strip(r0, lane0):
            halves = []
            for w_ref, b_ref, lo in ((cu_ref, bu_ref, lane0), (cg_ref, bg_ref, tn + lane0)):
                lanes = slice(lo, lo + V7X_LANES)
                wl = slice(lane0, lane0 + V7X_LANES)
                taps = [jnp.broadcast_to(w_ref[t:t + 1, wl], (sub, V7X_LANES)) for t in range(CONV_WIDTH)]
                bias = jnp.broadcast_to(b_ref[:, wl], (sub, V7X_LANES))
                window = [prev_ref[pl.ds(pl.multiple_of(r0 + t * sub, sub), sub), lanes]
                          for t in range(CONV_WIDTH - 1)]
                halves.append((lanes, taps, bias, window))
            pending = None
            for p in range(FFN_ROW_CHUNK // sub):
                vals = []
                for lanes, taps, bias, window in halves:
                    start = pl.multiple_of(r0 + (p + CONV_WIDTH - 1) * sub, sub)
                    window.append(prev_ref[pl.ds(start, sub), lanes])
                    acc = bias
                    for t in range(CONV_WIDTH):
                        acc = acc + taps[t] * window[t]
                    vals.append(acc)
                    del window[0]
                u, g = vals
                out = (0.5 * g * (1.0 + jnp.tanh(np.sqrt(2.0 / np.pi) * (g + 0.044715 * (g * g * g))))) * u
                if pending is None:
                    pending = out
                else:
                    rows = pl.ds(pl.multiple_of(r0 + (p - 1) * sub, 2 * sub), 2 * sub)
                    o_ref[rows, lane0:lane0 + V7X_LANES] = jnp.concatenate(
                        [pending, out], axis=0).astype(o_ref.dtype)
                    pending = None

        def chunk(c, carry):
            r0 = pl.multiple_of(c * FFN_ROW_CHUNK, FFN_ROW_CHUNK)
            cur_ref[pl.ds(r0 + head, FFN_ROW_CHUNK), :] = _dot(h_ref[pl.ds(r0, FFN_ROW_CHUNK), :],
                                                               wbf_ref[...])
            for lane0 in range(0, tn, V7X_LANES):
                strip(r0, lane0)
            return carry

        lax.fori_loop(0, tm // FFN_ROW_CHUNK, chunk, 0)

        own = pltpu.roll(cur_ref[tm:tm + head, :], 1, axis=0)
        before = jnp.where(matmul_starts_batch, 0.0,
                           pltpu.roll(prev_ref[tm:tm + head, :], 1 + V7X_SUBLANES, axis=0))
        first = lax.broadcasted_iota(jnp.int32, (head, 1), 0) % V7X_SUBLANES == 0
        cur_ref[0:head, :] = jnp.where(first, before, own)

    @pl.when(s % 2 == 0)
    def _():
        step(up_even_ref, up_odd_ref)

    @pl.when(s % 2 == 1)
    def _():
        step(up_odd_ref, up_even_ref)


def _ffn_up(h, w_up, w_conv, b_conv3, l, seq):
    m, d = h.shape
    f = w_up.shape[-1] // 2
    tm, tn = FFN_TILE, 512
    nt = f // tn
    row_tiles = m // tm
    n_pairs = nt * row_tiles
    kern = functools.partial(_ffn_up_kernel, row_tiles=row_tiles, tiles_per_batch=seq // tm,
                             n_pairs=n_pairs)
    mm = lambda s: jnp.minimum(s, n_pairs - 1)
    ep = lambda s: jnp.maximum(s - 1, 0)
    return pl.pallas_call(
        kern,
        grid=(n_pairs + 1,),
        in_specs=[pl.BlockSpec((tm, d), lambda s: (mm(s) % row_tiles, 0)),
                  pl.BlockSpec((None, d, tn), lambda s: (l, 0, mm(s) // row_tiles)),
                  pl.BlockSpec((None, d, tn), lambda s: (l, 0, nt + mm(s) // row_tiles)),
                  pl.BlockSpec((None, CONV_WIDTH, tn), lambda s: (l, 0, ep(s) // row_tiles)),
                  pl.BlockSpec((None, CONV_WIDTH, tn), lambda s: (l, 0, nt + ep(s) // row_tiles)),
                  pl.BlockSpec((None, 1, tn), lambda s: (l, 0, ep(s) // row_tiles)),
                  pl.BlockSpec((None, 1, tn), lambda s: (l, 0, nt + ep(s) // row_tiles))],
        out_specs=pl.BlockSpec((tm, tn), lambda s: (ep(s) % row_tiles, ep(s) // row_tiles)),
        out_shape=jax.ShapeDtypeStruct((m, f), BF16),
        scratch_shapes=[pltpu.VMEM((d, 2 * tn), BF16),
                        pltpu.VMEM((tm + 2 * V7X_SUBLANES, 2 * tn), F32),
                        pltpu.VMEM((tm + 2 * V7X_SUBLANES, 2 * tn), F32)],
        compiler_params=_params("arbitrary"),
        name="ffn_up_conv_gate",
    )(h, w_up, w_up, w_conv, w_conv, b_conv3, b_conv3)


def kernel(x, c, w_ada, b_ada, g_pre_mix, w_in, w_alpha_up, b_alpha, g_gla_head, w_gla_out,
           w_moba_out, w_o, g_post_mix, g_pre_ffn, w_up, w_conv, b_conv, w_down, g_post_ffn):
    batch, seq, d = x.shape
    depth = w_ada.shape[0]
    m = batch * seq
    gla_k = w_alpha_up.shape[-1]
    gla_heads = gla_k // GLA_DK
    gla_dv = g_gla_head.shape[-1]
    gla_v = w_gla_out.shape[1]
    moba_w = w_moba_out.shape[1]
    moba_heads = moba_w // MOBA_HD
    low_col = 2 * gla_k + 2 * gla_v
    assert gla_v == gla_heads * gla_dv and d % IN_TILE == 0
    assert low_col % IN_TILE == 0 and w_in.shape[-1] == low_col + GLA_RANK + 3 * moba_w + 2 * d
    first_shifted = low_col // IN_TILE
    n_tiles = first_shifted + 3 * moba_w // IN_TILE + 2 * d // IN_TILE

    x2 = x.reshape(m, d)
    c8 = jnp.pad(c, ((0, 8 - batch), (0, 0)))
    as3 = lambda a: a.reshape(a.shape[0], 1, a.shape[-1])
    wup_pad = jnp.pad(w_alpha_up, ((0, 0), (0, V7X_LANES - GLA_RANK), (0, 0)))
    w_in_t = jnp.swapaxes(w_in, 1, 2)

    for l in range(depth):
        mod = _modulation(c8, w_ada, as3(b_ada), l)
        mod3 = mod[:batch].reshape(batch, N_MOD, d)
        h, log_a = _prenorm(x2, as3(g_pre_mix), mod3, w_in_t, wup_pad, as3(b_alpha), l, seq,
                            shift_row=0, scale_row=1, low_block=low_col // V7X_LANES)
        proj = _in_projection(h, w_in_t, l, n_tiles, first_shifted)
        o_gla = _gla(proj, log_a, as3(g_gla_head), l, batch, seq, gla_heads, gla_dv)
        o_moba = _moba(proj, batch, seq, moba_heads)
        merged = _merge(o_gla, o_moba, w_gla_out, w_moba_out, proj, l, d,
                        gate_a0=first_shifted + 3 * moba_w // IN_TILE)
        y = _matmul(merged, w_o, l, 1024, 1024, "out_projection")
        x2, h2 = _post_mix(x2, y, as3(g_post_mix), mod3, as3(g_pre_ffn), l, seq)
        act = _ffn_up(h2, w_up, w_conv, as3(b_conv), l, seq)
        y2 = _matmul(act, w_down, l, 512, 512, "ffn_down")
        x2 = _post_ffn(x2, y2, as3(g_post_ffn), mod3, l, seq)
    return x2.reshape(batch, seq, d)
```

```python
import functools

import numpy as np
import jax
import jax.numpy as jnp
from jax import lax
from jax.experimental import pallas as pl
from jax.experimental.pallas import tpu as pltpu

F32 = jnp.float32
BF16 = jnp.bfloat16

GLA_DK = 128
GLA_RANK = 16
GLA_GATE_TEMP = 16.0
GLA_CHUNK = 64
GLA_CHUNKS_PER_TRIP = 4
MOBA_HD = 128
MOBA_BLOCK = 256
MOBA_TOPK = 3
MOBA_HEADS_PER_STEP = 2
CONV_WIDTH = 3
N_MOD = 6
EPS = 1e-6

V7X_LANES = 128
V7X_SUBLANES = 8
V7X_VMEM_LIMIT_BYTES = 56 * 1024 * 1024

IN_TILE = 1024
MASK_VALUE = -1e30
LOG2E = 1.4426950408889634
FFN_TILE = 512
FFN_ROW_CHUNK = 256


def _params(*semantics):
    return pltpu.CompilerParams(dimension_semantics=semantics,
                                vmem_limit_bytes=V7X_VMEM_LIMIT_BYTES)


def _dot(a, b):
    return jnp.dot(a, b, preferred_element_type=F32)


def _dot_nt(a, b):
    return lax.dot_general(a, b, (((1,), (1,)), ((), ())), preferred_element_type=F32)


def _dot_tn(a, b):
    return lax.dot_general(a, b, (((0,), (0,)), ((), ())), preferred_element_type=F32)


def _split3(x):
    x1 = x.astype(BF16)
    r1 = x - x1.astype(F32)
    x2 = r1.astype(BF16)
    x3 = (r1 - x2.astype(F32)).astype(BF16)
    return x1, x2, x3


def _rms(x, gain):
    return x * lax.rsqrt(jnp.mean(x * x, axis=-1, keepdims=True) + EPS) * gain


def _mod_kernel(c_ref, w_ref, b_ref, o_ref):
    c = c_ref[...]
    s = c * jax.nn.sigmoid(c)
    o_ref[...] = _dot(s.astype(BF16), w_ref[...].astype(BF16)) + b_ref[...]


def _modulation(c8, w_ada, b_ada3, l):
    rows, d = c8.shape
    n = w_ada.shape[-1]
    tn = 1024
    return pl.pallas_call(
        _mod_kernel,
        grid=(n // tn,),
        in_specs=[pl.BlockSpec((rows, d), lambda j: (0, 0)),
                  pl.BlockSpec((None, d, tn), lambda j: (l, 0, j)),
                  pl.BlockSpec((None, 1, tn), lambda j: (l, 0, j))],
        out_specs=pl.BlockSpec((rows, tn), lambda j: (0, j)),
        out_shape=jax.ShapeDtypeStruct((rows, n), F32),
        compiler_params=_params("arbitrary"),
        name="adaln_mod",
    )(c8, w_ada, b_ada3)


def _prenorm_kernel(x_ref, g_ref, mod_ref, w_ref, wup_ref, b_ref, o_ref, la_ref,
                    *, shift_row, scale_row):
    y = _rms(x_ref[...], g_ref[...])
    sh = mod_ref[shift_row:shift_row + 1, :]
    sc = mod_ref[scale_row:scale_row + 1, :]
    h = (y * (1.0 + sc) + sh).astype(o_ref.dtype)
    o_ref[...] = h
    low = _dot_nt(h, w_ref[...].astype(BF16))
    z = _dot(low.astype(BF16), wup_ref[...].astype(BF16)) + b_ref[...]
    log_sig = jnp.minimum(z, 0.0) - jnp.log1p(jnp.exp(-jnp.abs(z)))
    la_ref[...] = log_sig / GLA_GATE_TEMP


def _prenorm(x2, gain3, mod3, w_in_t, wup_pad, b_alpha3, l, seq, shift_row, scale_row, low_block):
    m, d = x2.shape
    k = wup_pad.shape[-1]
    tm = 512
    per_batch = seq // tm
    return pl.pallas_call(
        functools.partial(_prenorm_kernel, shift_row=shift_row, scale_row=scale_row),
        grid=(m // tm,),
        in_specs=[pl.BlockSpec((tm, d), lambda i: (i, 0)),
                  pl.BlockSpec((None, 1, d), lambda i: (l, 0, 0)),
                  pl.BlockSpec((None, N_MOD, d), lambda i: (i // per_batch, 0, 0)),
                  pl.BlockSpec((None, V7X_LANES, d), lambda i: (l, low_block, 0)),
                  pl.BlockSpec((None, V7X_LANES, k), lambda i: (l, 0, 0)),
                  pl.BlockSpec((None, 1, k), lambda i: (l, 0, 0))],
        out_specs=(pl.BlockSpec((tm, d), lambda i: (i, 0)),
                   pl.BlockSpec((tm, k), lambda i: (i, 0))),
        out_shape=(jax.ShapeDtypeStruct((m, d), BF16), jax.ShapeDtypeStruct((m, k), F32)),
        compiler_params=_params("arbitrary"),
        name="prenorm_mix",
    )(x2, gain3, mod3, w_in_t, wup_pad, b_alpha3)


def _inproj_kernel(h_ref, wa_ref, wb_ref, o_ref, wbf_ref, *, first_shifted):
    j = pl.program_id(0)
    i = pl.program_id(1)
    keep = IN_TILE - GLA_RANK

    @pl.when(jnp.logical_and(i == 0, j < first_shifted))
    def _():
        wbf_ref[...] = wa_ref[...].astype(BF16)

    @pl.when(jnp.logical_and(i == 0, j >= first_shifted))
    def _():
        wbf_ref[:keep, :] = wa_ref[GLA_RANK:, :].astype(BF16)
        wbf_ref[keep:, :] = wb_ref[...].astype(BF16)

    o_ref[...] = _dot_nt(h_ref[...], wbf_ref[...]).astype(o_ref.dtype)


def _in_projection(h, w_in_t, l, n_tiles, first_shifted):
    m, d = h.shape
    tm = 1024
    rank_blocks = IN_TILE // GLA_RANK
    return pl.pallas_call(
        functools.partial(_inproj_kernel, first_shifted=first_shifted),
        grid=(n_tiles, m // tm),
        in_specs=[pl.BlockSpec((tm, d), lambda j, i: (i, 0)),
                  pl.BlockSpec((None, IN_TILE, d), lambda j, i: (l, j, 0)),
                  pl.BlockSpec((None, GLA_RANK, d), lambda j, i: (l, (j + 1) * rank_blocks, 0))],
        out_specs=pl.BlockSpec((None, tm, IN_TILE), lambda j, i: (j, i, 0)),
        out_shape=jax.ShapeDtypeStruct((n_tiles, m, IN_TILE), BF16),
        scratch_shapes=[pltpu.VMEM((IN_TILE, d), BF16)],
        compiler_params=_params("arbitrary", "arbitrary"),
        name="in_projection",
    )(h, w_in_t, w_in_t)


def _gla_tables(c):
    levels = int(np.log2(c))
    t = np.arange(c)
    tabs = [(t[None, :] <= t[:, None]),
            (t[None, :] > t[:, None])]
    for lv in range(1, levels + 1):
        size, half = 1 << lv, 1 << (lv - 1)
        mid = (t // size) * size + half
        second = t >= mid
        q_side = second[:, None] & (t[None, :] >= mid[:, None]) & (t[None, :] <= t[:, None])
        k_side = (~second)[:, None] & (t[None, :] > t[:, None]) & (t[None, :] < mid[:, None])
        tabs.append(q_side | k_side)
    e = np.concatenate(tabs, axis=0).astype(np.float32)
    x = t[:, None] ^ t[None, :]
    lvl = np.where(x == 0, 0, np.floor(np.log2(np.maximum(x, 1))).astype(np.int64) + 1)
    lvl = np.where(t[:, None] >= t[None, :], lvl, -1).astype(np.int32)
    return e, lvl, levels


def _gla_kernel(q_ref, k_ref, v_ref, r_ref, la_ref, gh_ref, e_ref, lvl_ref, o_ref, st_ref,
                *ex_refs, heads, dv, levels, n_chunks):
    c = GLA_CHUNK
    dk = GLA_DK

    @pl.when(pl.program_id(1) == 0)
    def _():
        st_ref[...] = jnp.zeros_like(st_ref)

    e_tab = e_ref[...]
    lvl = lvl_ref[...]
    row = lax.broadcasted_iota(jnp.int32, (c, 1), 0)
    gain = gh_ref[...]

    def chunk(ci, ex_ref):
        rows = pl.ds(pl.multiple_of(ci * c, c), c)
        g3 = jnp.concatenate(_split3(la_ref[rows, :]), axis=0)
        ex_ref[...] = jnp.exp(_dot(e_tab, g3))
        for h in range(heads):
            sl = slice(h * dk, (h + 1) * dk)
            vs = slice(h * dv, (h + 1) * dv)
            qh = q_ref[rows, sl].astype(F32) * (dk ** -0.5)
            kh = k_ref[rows, sl].astype(F32)
            vh = v_ref[rows, vs]
            st = st_ref[h]
            o = _dot_nt((qh * ex_ref[0:c, sl]).astype(BF16), st.astype(BF16))
            a = jnp.where(lvl == 0, _dot_nt(qh.astype(BF16), kh.astype(BF16)), 0.0)
            for lv in range(1, levels + 1):
                e_lv = ex_ref[(1 + lv) * c:(2 + lv) * c, sl]
                second = ((row >> (lv - 1)) & 1) == 1
                ql = jnp.where(second, qh * e_lv, 0.0).astype(BF16)
                kl = jnp.where(second, 0.0, kh * e_lv).astype(BF16)
                a = a + jnp.where(lvl == lv, _dot_nt(ql, kl), 0.0)
            o = o + _dot(a.astype(BF16), vh)
            kt = (kh * ex_ref[c:2 * c, sl]).astype(BF16)
            st_ref[h] = st * ex_ref[c - 1:c, sl] + _dot_tn(vh, kt)
            y = _rms(o, gain)
            r = r_ref[rows, vs].astype(F32)
            o_ref[rows, vs] = (y * (r * jax.nn.sigmoid(r))).astype(o_ref.dtype)

    def chunk_group(gi, carry):
        for u in range(GLA_CHUNKS_PER_TRIP):
            chunk(gi * GLA_CHUNKS_PER_TRIP + u, ex_refs[u])
        return carry

    lax.fori_loop(0, n_chunks // GLA_CHUNKS_PER_TRIP, chunk_group, 0)


def _gla(proj, log_a, g_head3, l, batch, seq, heads, dv):
    n_t, m, _ = proj.shape
    dk = GLA_DK
    rows = 512
    per_batch = seq // rows
    e_np, lvl_np, levels = _gla_tables(GLA_CHUNK)
    e_np = np.concatenate([e_np, e_np, e_np], axis=1)
    e_tab = jnp.asarray(e_np, BF16)
    lvl = jnp.asarray(lvl_np)
    kw = heads * dk
    assert 2 * kw == IN_TILE and heads * dv == IN_TILE
    kern = functools.partial(_gla_kernel, heads=heads, dv=dv, levels=levels,
                             n_chunks=rows // GLA_CHUNK)
    return pl.pallas_call(
        kern,
        grid=(batch, per_batch),
        in_specs=[pl.BlockSpec((None, rows, kw), lambda b, r: (0, b * per_batch + r, 0)),
                  pl.BlockSpec((None, rows, kw), lambda b, r: (0, b * per_batch + r, 1)),
                  pl.BlockSpec((None, rows, IN_TILE), lambda b, r: (1, b * per_batch + r, 0)),
                  pl.BlockSpec((None, rows, IN_TILE), lambda b, r: (2, b * per_batch + r, 0)),
                  pl.BlockSpec((rows, kw), lambda b, r: (b * per_batch + r, 0)),
                  pl.BlockSpec((None, 1, dv), lambda b, r: (l, 0, 0)),
                  pl.BlockSpec(e_np.shape, lambda b, r: (0, 0)),
                  pl.BlockSpec(lvl_np.shape, lambda b, r: (0, 0))],
        out_specs=pl.BlockSpec((rows, IN_TILE), lambda b, r: (b * per_batch + r, 0)),
        out_shape=jax.ShapeDtypeStruct((m, IN_TILE), BF16),
        scratch_shapes=[pltpu.VMEM((heads, dv, dk), F32)]
        + [pltpu.VMEM((e_np.shape[0], kw), F32)] * GLA_CHUNKS_PER_TRIP,
        compiler_params=_params("arbitrary", "arbitrary"),
        name="gla",
    )(proj, proj, proj, proj, log_a, g_head3, e_tab, lvl)


def _moba_kernel(q_ref, k_ref, v_ref, o_ref, kmean_ref, vt_ref, causal_ref, *tile_state,
                 n_blocks, topk, heads):
    blk, hd = MOBA_BLOCK, MOBA_HD
    i = pl.program_id(2)
    c2 = (hd ** -0.5) * LOG2E
    head_lanes = [slice(h * hd, (h + 1) * hd) for h in range(heads)]
    per_tile = len(tile_state) // 2
    even, odd = tile_state[:per_tile], tile_state[per_tile:]
    has_tile = i < n_blocks

    @pl.when(i == 0)
    def _():
        for h, lanes in enumerate(head_lanes):
            for n in range(n_blocks):
                rows = slice(n * blk, (n + 1) * blk)
                kmean_ref[h, n:n + 1, :] = jnp.mean(k_ref[rows, lanes].astype(F32), axis=0, keepdims=True)
                vt_ref[h, :, rows] = v_ref[rows, lanes].astype(F32).T.astype(BF16)

        kpos = lax.broadcasted_iota(jnp.int32, (blk, blk), 0)
        qpos = lax.broadcasted_iota(jnp.int32, (blk, blk), 1)
        causal_ref[...] = jnp.where(kpos <= qpos, 0.0, MASK_VALUE)
        for _, mblk_ref, _, _, l_ref, acc_ref in (even, odd):
            mblk_ref[...] = jnp.zeros_like(mblk_ref)
            l_ref[...] = jnp.ones_like(l_ref)
            acc_ref[...] = jnp.zeros_like(acc_ref)

    def pairwise(n, body):
        def two(t, carry):
            body(2 * t)
            body(2 * t + 1)
            return carry

        lax.fori_loop(0, lax.shift_right_logical(n, 1), two, 0)

        @pl.when(n % 2 == 1)
        def _():
            body(n - 1)

    def step(this, last):
        bias_ref, mblk_ref, s_ref, m_ref, l_ref, acc_ref = this
        bias_last, _, s_last, m_last, l_last, acc_last = last

        @pl.when(has_tile)
        def _():
            for h, lanes in enumerate(head_lanes):
                q = q_ref[:, lanes]
                m1, m2, m3 = _split3(kmean_ref[h])
                s_blk = _dot_nt(m1, q) + _dot_nt(m2, q) + _dot_nt(m3, q)
                bidx = lax.broadcasted_iota(jnp.int32, s_blk.shape, 0)
                cur = jnp.where(bidx < i, s_blk, MASK_VALUE)
                sel = jnp.zeros(s_blk.shape, jnp.bool_)
                for _ in range(topk):
                    mx = jnp.max(cur, axis=0, keepdims=True)
                    idx = jnp.min(jnp.where(cur == mx, bidx, n_blocks), axis=0, keepdims=True)
                    pick = bidx == idx
                    sel = jnp.logical_or(sel, jnp.logical_and(pick, idx < i))
                    cur = jnp.where(pick, -jnp.inf, cur)
                bias_ref[h] = jnp.where(jnp.logical_or(sel, bidx == i), 0.0, MASK_VALUE)

        def score(j, causal):
            cols = pl.ds(pl.multiple_of(j * blk, blk), blk)
            for h, lanes in enumerate(head_lanes):
                s = _dot_nt(k_ref[cols, lanes], q_ref[:, lanes]) * c2
                if causal:
                    s = s + causal_ref[...]
                s_ref[h, j] = s
                mblk_ref[h, pl.ds(j, 1), :] = jnp.max(s, axis=0, keepdims=True)

        def attend(j):
            cols = pl.ds(pl.multiple_of(j * blk, blk), blk)
            for h in range(heads):
                p = jnp.exp2(s_last[h, j] + (bias_last[h, pl.ds(j, 1), :] - m_last[h]))
                l_last[h] += jnp.sum(p, axis=0, keepdims=True)
                acc_last[h] += _dot(vt_ref[h, :, cols], p.astype(BF16))

        def both(j):
            score(j, False)
            attend(j)

        @pl.when(has_tile)
        def _():
            pairwise(i, both)
            score(i, True)
            for h in range(heads):
                bidx = lax.broadcasted_iota(jnp.int32, (n_blocks, blk), 0)
                allowed_max = jnp.where(bidx <= i, mblk_ref[h] + bias_ref[h], -jnp.inf)
                m_ref[h] = jnp.max(allowed_max, axis=0, keepdims=True)
                l_ref[h] = jnp.zeros((1, blk), F32)
                acc_ref[h] = jnp.zeros((hd, blk), F32)

        @pl.when(jnp.logical_not(has_tile))
        def _():
            pairwise(i, attend)

        for h, lanes in enumerate(head_lanes):
            o_ref[:, lanes] = (acc_last[h] / l_last[h]).T.astype(o_ref.dtype)

    @pl.when(i % 2 == 0)
    def _():
        step(even, odd)

    @pl.when(i % 2 == 1)
    def _():
        step(odd, even)


def _moba(proj, batch, seq, heads):
    n_t, m, _ = proj.shape
    hd, blk = MOBA_HD, MOBA_BLOCK
    hp = MOBA_HEADS_PER_STEP
    assert seq % blk == 0 and heads * hd == IN_TILE and heads % hp == 0
    n_blocks = seq // blk
    topk = min(MOBA_TOPK, n_blocks - 1)
    wide = hp * hd
    kern = functools.partial(_moba_kernel, n_blocks=n_blocks, topk=topk, heads=hp)
    tile_state = [pltpu.VMEM((hp, n_blocks, blk), F32),
                  pltpu.VMEM((hp, n_blocks, blk), F32),
                  pltpu.VMEM((hp, n_blocks, blk, blk), F32),
                  pltpu.VMEM((hp, 1, blk), F32),
                  pltpu.VMEM((hp, 1, blk), F32),
                  pltpu.VMEM((hp, hd, blk), F32)]
    q_tile = lambda i: jnp.minimum(i, n_blocks - 1)
    o_tile = lambda i: jnp.maximum(i - 1, 0)
    return pl.pallas_call(
        kern,
        grid=(batch, heads // hp, n_blocks + 1),
        in_specs=[pl.BlockSpec((None, blk, wide), lambda b, g, i: (3, b * n_blocks + q_tile(i), g)),
                  pl.BlockSpec((None, seq, wide), lambda b, g, i: (4, b, g)),
                  pl.BlockSpec((None, seq, wide), lambda b, g, i: (5, b, g))],
        out_specs=pl.BlockSpec((blk, wide), lambda b, g, i: (b * n_blocks + o_tile(i), g)),
        out_shape=jax.ShapeDtypeStruct((m, IN_TILE), BF16),
        scratch_shapes=[pltpu.VMEM((hp, n_blocks, hd), F32),
                        pltpu.VMEM((hp, hd, seq), BF16),
                        pltpu.VMEM((blk, blk), F32)] + tile_state + tile_state,
        compiler_params=_params("arbitrary", "arbitrary", "arbitrary"),
        name="moba",
    )(proj, proj, proj)


def _merge_kernel(og_ref, om_ref, wg_ref, wm_ref, ga_ref, gb_ref, o_ref, wgb_ref, wmb_ref):
    @pl.when(pl.program_id(1) == 0)
    def _():
        wgb_ref[...] = wg_ref[...].astype(BF16)
        wmb_ref[...] = wm_ref[...].astype(BF16)

    ya = _dot(og_ref[...], wgb_ref[...])
    yb = _dot(om_ref[...], wmb_ref[...])
    ga = jax.nn.sigmoid(ga_ref[...].astype(F32))
    gb = jax.nn.sigmoid(gb_ref[...].astype(F32))
    o_ref[...] = (ga * ya + gb * yb).astype(o_ref.dtype)


def _merge(o_gla, o_moba, w_gla_out, w_moba_out, proj, l, d, gate_a0):
    m, kg = o_gla.shape
    km = o_moba.shape[1]
    tm, tn = 1024, IN_TILE
    n_tiles = d // tn
    gate_b0 = gate_a0 + n_tiles
    return pl.pallas_call(
        _merge_kernel,
        grid=(n_tiles, m // tm),
        in_specs=[pl.BlockSpec((tm, kg), lambda j, i: (i, 0)),
                  pl.BlockSpec((tm, km), lambda j, i: (i, 0)),
                  pl.BlockSpec((None, kg, tn), lambda j, i: (l, 0, j)),
                  pl.BlockSpec((None, km, tn), lambda j, i: (l, 0, j)),
                  pl.BlockSpec((None, tm, tn), lambda j, i: (gate_a0 + j, i, 0)),
                  pl.BlockSpec((None, tm, tn), lambda j, i: (gate_b0 + j, i, 0))],
        out_specs=pl.BlockSpec((tm, tn), lambda j, i: (i, j)),
        out_shape=jax.ShapeDtypeStruct((m, d), BF16),
        scratch_shapes=[pltpu.VMEM((kg, tn), BF16), pltpu.VMEM((km, tn), BF16)],
        compiler_params=_params("arbitrary", "arbitrary"),
        name="branch_merge",
    )(o_gla, o_moba, w_gla_out, w_moba_out, proj, proj)


def _matmul_kernel(a_ref, w_ref, o_ref, wbf_ref):
    @pl.when(pl.program_id(1) == 0)
    def _():
        wbf_ref[...] = w_ref[...].astype(BF16)

    o_ref[...] = _dot(a_ref[...], wbf_ref[...]).astype(o_ref.dtype)


def _matmul(a, w, l, tm, tn, name):
    m, k = a.shape
    n = w.shape[-1]
    return pl.pallas_call(
        _matmul_kernel,
        grid=(n // tn, m // tm),
        in_specs=[pl.BlockSpec((tm, k), lambda j, i: (i, 0)),
                  pl.BlockSpec((None, k, tn), lambda j, i: (l, 0, j))],
        out_specs=pl.BlockSpec((tm, tn), lambda j, i: (i, j)),
        out_shape=jax.ShapeDtypeStruct((m, n), BF16),
        scratch_shapes=[pltpu.VMEM((k, tn), BF16)],
        compiler_params=_params("arbitrary", "arbitrary"),
        name=name,
    )(a, w)


def _ffn_row_permutation(tile):
    g = tile // V7X_SUBLANES
    pos = np.arange(tile)
    time = (pos % V7X_SUBLANES) * g + pos // V7X_SUBLANES
    perm = np.zeros((tile, tile), np.float32)
    perm[pos, time] = 1.0
    return perm


def _post_mix_kernel(x_ref, y_ref, gpost_ref, mod_ref, gpre_ref, perm_ref, o_ref, h_ref,
                     *, gate_row, shift_row, scale_row):
    y = y_ref[...].astype(F32)
    x = x_ref[...] + mod_ref[gate_row:gate_row + 1, :] * _rms(y, gpost_ref[...])
    o_ref[...] = x
    h = _rms(x, gpre_ref[...])
    h = h * (1.0 + mod_ref[scale_row:scale_row + 1, :]) + mod_ref[shift_row:shift_row + 1, :]
    h_ref[...] = _dot(perm_ref[...], h.astype(BF16)).astype(h_ref.dtype)


def _post_mix(x2, y, gpost3, mod3, gpre3, l, seq):
    m, d = x2.shape
    tm = FFN_TILE
    per_batch = seq // tm
    row_spec = pl.BlockSpec((tm, d), lambda i: (i, 0))
    gain_spec = pl.BlockSpec((None, 1, d), lambda i: (l, 0, 0))
    perm = jnp.asarray(_ffn_row_permutation(tm), BF16)
    return pl.pallas_call(
        functools.partial(_post_mix_kernel, gate_row=2, shift_row=3, scale_row=4),
        grid=(m // tm,),
        in_specs=[row_spec, row_spec, gain_spec,
                  pl.BlockSpec((None, N_MOD, d), lambda i: (i // per_batch, 0, 0)),
                  gain_spec,
                  pl.BlockSpec((tm, tm), lambda i: (0, 0))],
        out_specs=(row_spec, row_spec),
        out_shape=(jax.ShapeDtypeStruct((m, d), F32), jax.ShapeDtypeStruct((m, d), BF16)),
        compiler_params=_params("arbitrary"),
        name="post_mix_pre_ffn",
    )(x2, y, gpost3, mod3, gpre3, perm)


def _post_ffn_kernel(x_ref, y_ref, gpost_ref, mod_ref, unperm_ref, o_ref, *, gate_row):
    y = _dot(unperm_ref[...], y_ref[...])
    o_ref[...] = x_ref[...] + mod_ref[gate_row:gate_row + 1, :] * _rms(y, gpost_ref[...])


def _post_ffn(x2, y, gpost3, mod3, l, seq):
    m, d = x2.shape
    tm = FFN_TILE
    per_batch = seq // tm
    row_spec = pl.BlockSpec((tm, d), lambda i: (i, 0))
    unperm = jnp.asarray(_ffn_row_permutation(tm).T, BF16)
    return pl.pallas_call(
        functools.partial(_post_ffn_kernel, gate_row=5),
        grid=(m // tm,),
        in_specs=[row_spec, row_spec,
                  pl.BlockSpec((None, 1, d), lambda i: (l, 0, 0)),
                  pl.BlockSpec((None, N_MOD, d), lambda i: (i // per_batch, 0, 0)),
                  pl.BlockSpec((tm, tm), lambda i: (0, 0))],
        out_specs=row_spec,
        out_shape=jax.ShapeDtypeStruct((m, d), F32),
        compiler_params=_params("arbitrary"),
        name="post_ffn",
    )(x2, y, gpost3, mod3, unperm)


def _ffn_up_kernel(h_ref, wu_ref, wg_ref, cu_ref, cg_ref, bu_ref, bg_ref, o_ref,
                   wbf_ref, up_even_ref, up_odd_ref, *, row_tiles, tiles_per_batch, n_pairs):
    s = pl.program_id(0)
    tm, tn = o_ref.shape
    head = 2 * V7X_SUBLANES
    matmul_starts_batch = (s % row_tiles) % tiles_per_batch == 0

    @pl.when(s == 0)
    def _():
        up_even_ref[...] = jnp.zeros_like(up_even_ref)
        up_odd_ref[...] = jnp.zeros_like(up_odd_ref)

    @pl.when(jnp.logical_and(s % row_tiles == 0, s < n_pairs))
    def _():
        wbf_ref[:, :tn] = wu_ref[...].astype(BF16)
        wbf_ref[:, tn:] = wg_ref[...].astype(BF16)

    def step(cur_ref, prev_ref):
        sub = V7X_SUBLANES

        def epilogue(r0):
            halves = []
            for w_ref, b_ref, lo in ((cu_ref, bu_ref, 0), (cg_ref, bg_ref, tn)):
                cols = slice(lo, lo + tn)
                window = [prev_ref[pl.ds(pl.multiple_of(r0 + t * sub, sub), sub), cols]
                          for t in range(CONV_WIDTH - 1)]
                halves.append((w_ref, b_ref, cols, window))
            pending = None
            for p in range(FFN_ROW_CHUNK // sub):
                vals = []
                for w_ref, b_ref, cols, window in halves:
                    start = pl.multiple_of(r0 + (p + CONV_WIDTH - 1) * sub, sub)
                    window.append(prev_ref[pl.ds(start, sub), cols])
                    acc = b_ref[...]
                    for t in range(CONV_WIDTH):
                        acc = acc + w_ref[t:t + 1, :] * window[t]
                    vals.append(acc)
                    del window[0]
                u, g = vals
                out = (0.5 * g * (1.0 + jnp.tanh(np.sqrt(2.0 / np.pi) * (g + 0.044715 * (g * g * g))))) * u
                if pending is None:
                    pending = out
                else:
                    rows = pl.ds(pl.multiple_of(r0 + (p - 1) * sub, 2 * sub), 2 * sub)
                    o_ref[rows, :] = jnp.concatenate([pending, out], axis=0).astype(o_ref.dtype)
                    pending = None

        def chunk(c, carry):
            r0 = pl.multiple_of(c * FFN_ROW_CHUNK, FFN_ROW_CHUNK)
            cur_ref[pl.ds(r0 + head, FFN_ROW_CHUNK), :] = _dot(h_ref[pl.ds(r0, FFN_ROW_CHUNK), :],
                                                               wbf_ref[...])
            epilogue(r0)
            return carry

        lax.fori_loop(0, tm // FFN_ROW_CHUNK, chunk, 0)

        own = pltpu.roll(cur_ref[tm:tm + head, :], 1, axis=0)
        before = jnp.where(matmul_starts_batch, 0.0,
                           pltpu.roll(prev_ref[tm:tm + head, :], 1 + V7X_SUBLANES, axis=0))
        first = lax.broadcasted_iota(jnp.int32, (head, 1), 0) % V7X_SUBLANES == 0
        cur_ref[0:head, :] = jnp.where(first, before, own)

    @pl.when(s % 2 == 0)
    def _():
        step(up_even_ref, up_odd_ref)

    @pl.when(s % 2 == 1)
    def _():
        step(up_odd_ref, up_even_ref)


def _ffn_up(h, w_up, w_conv, b_conv3, l, seq):
    m, d = h.shape
    f = w_up.shape[-1] // 2
    tm, tn = FFN_TILE, 512
    nt = f // tn
    row_tiles = m // tm
    n_pairs = nt * row_tiles
    kern = functools.partial(_ffn_up_kernel, row_tiles=row_tiles, tiles_per_batch=seq // tm,
                             n_pairs=n_pairs)
    mm = lambda s: jnp.minimum(s, n_pairs - 1)
    ep = lambda s: jnp.maximum(s - 1, 0)
    return pl.pallas_call(
        kern,
        grid=(n_pairs + 1,),
        in_specs=[pl.BlockSpec((tm, d), lambda s: (mm(s) % row_tiles, 0)),
                  pl.BlockSpec((None, d, tn), lambda s: (l, 0, mm(s) // row_tiles)),
                  pl.BlockSpec((None, d, tn), lambda s: (l, 0, nt + mm(s) // row_tiles)),
                  pl.BlockSpec((None, CONV_WIDTH, tn), lambda s: (l, 0, ep(s) // row_tiles)),
                  pl.BlockSpec((None, CONV_WIDTH, tn), lambda s: (l, 0, nt + ep(s) // row_tiles)),
                  pl.BlockSpec((None, 1, tn), lambda s: (l, 0, ep(s) // row_tiles)),
                  pl.BlockSpec((None, 1, tn), lambda s: (l, 0, nt + ep(s) // row_tiles))],
        out_specs=pl.BlockSpec((tm, tn), lambda s: (ep(s) % row_tiles, ep(s) // row_tiles)),
        out_shape=jax.ShapeDtypeStruct((m, f), BF16),
        scratch_shapes=[pltpu.VMEM((d, 2 * tn), BF16),
                        pltpu.VMEM((tm + 2 * V7X_SUBLANES, 2 * tn), F32),
                        pltpu.VMEM((tm + 2 * V7X_SUBLANES, 2 * tn), F32)],
        compiler_params=_params("arbitrary"),
        name="ffn_up_conv_gate",
    )(h, w_up, w_up, w_conv, w_conv, b_conv3, b_conv3)


def kernel(x, c, w_ada, b_ada, g_pre_mix, w_in, w_alpha_up, b_alpha, g_gla_head, w_gla_out,
           w_moba_out, w_o, g_post_mix, g_pre_ffn, w_up, w_conv, b_conv, w_down, g_post_ffn):
    batch, seq, d = x.shape
    depth = w_ada.shape[0]
    m = batch * seq
    gla_k = w_alpha_up.shape[-1]
    gla_heads = gla_k // GLA_DK
    gla_dv = g_gla_head.shape[-1]
    gla_v = w_gla_out.shape[1]
    moba_w = w_moba_out.shape[1]
    moba_heads = moba_w // MOBA_HD
    low_col = 2 * gla_k + 2 * gla_v
    assert gla_v == gla_heads * gla_dv and d % IN_TILE == 0
    assert low_col % IN_TILE == 0 and w_in.shape[-1] == low_col + GLA_RANK + 3 * moba_w + 2 * d
    first_shifted = low_col // IN_TILE
    n_tiles = first_shifted + 3 * moba_w // IN_TILE + 2 * d // IN_TILE

    x2 = x.reshape(m, d)
    c8 = jnp.pad(c, ((0, 8 - batch), (0, 0)))
    as3 = lambda a: a.reshape(a.shape[0], 1, a.shape[-1])
    wup_pad = jnp.pad(w_alpha_up, ((0, 0), (0, V7X_LANES - GLA_RANK), (0, 0)))
    w_in_t = jnp.swapaxes(w_in, 1, 2)

    for l in range(depth):
        mod = _modulation(c8, w_ada, as3(b_ada), l)
        mod3 = mod[:batch].reshape(batch, N_MOD, d)
        h, log_a = _prenorm(x2, as3(g_pre_mix), mod3, w_in_t, wup_pad, as3(b_alpha), l, seq,
                            shift_row=0, scale_row=1, low_block=low_col // V7X_LANES)
        proj = _in_projection(h, w_in_t, l, n_tiles, first_shifted)
        o_gla = _gla(proj, log_a, as3(g_gla_head), l, batch, seq, gla_heads, gla_dv)
        o_moba = _moba(proj, batch, seq, moba_heads)
        merged = _merge(o_gla, o_moba, w_gla_out, w_moba_out, proj, l, d,
                        gate_a0=first_shifted + 3 * moba_w // IN_TILE)
        y = _matmul(merged, w_o, l, 1024, 1024, "out_projection")
        x2, h2 = _post_mix(x2, y, as3(g_post_mix), mod3, as3(g_pre_ffn), l, seq)
        act = _ffn_up(h2, w_up, w_conv, as3(b_conv), l, seq)
        y2 = _matmul(act, w_down, l, 512, 512, "ffn_down")
        x2 = _post_ffn(x2, y2, as3(g_post_ffn), mod3, l, seq)
    return x2.reshape(batch, seq, d)
```

```python
import functools

import numpy as np
import jax
import jax.numpy as jnp
from jax import lax
from jax.experimental import pallas as pl
from jax.experimental.pallas import tpu as pltpu

F32 = jnp.float32
BF16 = jnp.bfloat16

GLA_DK = 128
GLA_RANK = 16
GLA_GATE_TEMP = 16.0
GLA_CHUNK = 64
GLA_CHUNKS_PER_TRIP = 4
MOBA_HD = 128
MOBA_BLOCK = 256
MOBA_TOPK = 3
MOBA_HEADS_PER_STEP = 2
CONV_WIDTH = 3
N_MOD = 6
EPS = 1e-6

V7X_LANES = 128
V7X_SUBLANES = 8
V7X_VMEM_LIMIT_BYTES = 56 * 1024 * 1024

IN_TILE = 1024
MASK_VALUE = -1e30
LOG2E = 1.4426950408889634
FFN_TILE = 512
FFN_ROW_CHUNK = 256
FFN_ROW_GROUP = 16
FFN_PAD_LANES = 128


def _params(*semantics):
    return pltpu.CompilerParams(dimension_semantics=semantics,
                                vmem_limit_bytes=V7X_VMEM_LIMIT_BYTES)


def _dot(a, b):
    return jnp.dot(a, b, preferred_element_type=F32)


def _dot_nt(a, b):
    return lax.dot_general(a, b, (((1,), (1,)), ((), ())), preferred_element_type=F32)


def _dot_tn(a, b):
    return lax.dot_general(a, b, (((0,), (0,)), ((), ())), preferred_element_type=F32)


def _split3(x):
    x1 = x.astype(BF16)
    r1 = x - x1.astype(F32)
    x2 = r1.astype(BF16)
    x3 = (r1 - x2.astype(F32)).astype(BF16)
    return x1, x2, x3


def _rms(x, gain):
    return x * lax.rsqrt(jnp.mean(x * x, axis=-1, keepdims=True) + EPS) * gain


def _mod_kernel(c_ref, w_ref, b_ref, o_ref):
    c = c_ref[...]
    s = c * jax.nn.sigmoid(c)
    o_ref[...] = _dot(s.astype(BF16), w_ref[...].astype(BF16)) + b_ref[...]


def _modulation(c8, w_ada, b_ada3, l):
    rows, d = c8.shape
    n = w_ada.shape[-1]
    tn = 1024
    return pl.pallas_call(
        _mod_kernel,
        grid=(n // tn,),
        in_specs=[pl.BlockSpec((rows, d), lambda j: (0, 0)),
                  pl.BlockSpec((None, d, tn), lambda j: (l, 0, j)),
                  pl.BlockSpec((None, 1, tn), lambda j: (l, 0, j))],
        out_specs=pl.BlockSpec((rows, tn), lambda j: (0, j)),
        out_shape=jax.ShapeDtypeStruct((rows, n), F32),
        compiler_params=_params("arbitrary"),
        name="adaln_mod",
    )(c8, w_ada, b_ada3)


def _prenorm_kernel(x_ref, g_ref, mod_ref, w_ref, wup_ref, b_ref, o_ref, la_ref,
                    *, shift_row, scale_row):
    y = _rms(x_ref[...], g_ref[...])
    sh = mod_ref[shift_row:shift_row + 1, :]
    sc = mod_ref[scale_row:scale_row + 1, :]
    h = (y * (1.0 + sc) + sh).astype(o_ref.dtype)
    o_ref[...] = h
    low = _dot_nt(h, w_ref[...].astype(BF16))
    z = _dot(low.astype(BF16), wup_ref[...].astype(BF16)) + b_ref[...]
    log_sig = jnp.minimum(z, 0.0) - jnp.log1p(jnp.exp(-jnp.abs(z)))
    la_ref[...] = log_sig / GLA_GATE_TEMP


def _prenorm(x2, gain3, mod3, w_in_t, wup_pad, b_alpha3, l, seq, shift_row, scale_row, low_block):
    m, d = x2.shape
    k = wup_pad.shape[-1]
    tm = 512
    per_batch = seq // tm
    return pl.pallas_call(
        functools.partial(_prenorm_kernel, shift_row=shift_row, scale_row=scale_row),
        grid=(m // tm,),
        in_specs=[pl.BlockSpec((tm, d), lambda i: (i, 0)),
                  pl.BlockSpec((None, 1, d), lambda i: (l, 0, 0)),
                  pl.BlockSpec((None, N_MOD, d), lambda i: (i // per_batch, 0, 0)),
                  pl.BlockSpec((None, V7X_LANES, d), lambda i: (l, low_block, 0)),
                  pl.BlockSpec((None, V7X_LANES, k), lambda i: (l, 0, 0)),
                  pl.BlockSpec((None, 1, k), lambda i: (l, 0, 0))],
        out_specs=(pl.BlockSpec((tm, d), lambda i: (i, 0)),
                   pl.BlockSpec((tm, k), lambda i: (i, 0))),
        out_shape=(jax.ShapeDtypeStruct((m, d), BF16), jax.ShapeDtypeStruct((m, k), F32)),
        compiler_params=_params("arbitrary"),
        name="prenorm_mix",
    )(x2, gain3, mod3, w_in_t, wup_pad, b_alpha3)


def _inproj_kernel(h_ref, wa_ref, wb_ref, o_ref, wbf_ref, *, first_shifted):
    j = pl.program_id(0)
    i = pl.program_id(1)
    keep = IN_TILE - GLA_RANK

    @pl.when(jnp.logical_and(i == 0, j < first_shifted))
    def _():
        wbf_ref[...] = wa_ref[...].astype(BF16)

    @pl.when(jnp.logical_and(i == 0, j >= first_shifted))
    def _():
        wbf_ref[:keep, :] = wa_ref[GLA_RANK:, :].astype(BF16)
        wbf_ref[keep:, :] = wb_ref[...].astype(BF16)

    o_ref[...] = _dot_nt(h_ref[...], wbf_ref[...]).astype(o_ref.dtype)


def _in_projection(h, w_in_t, l, n_tiles, first_shifted):
    m, d = h.shape
    tm = 1024
    rank_blocks = IN_TILE // GLA_RANK
    return pl.pallas_call(
        functools.partial(_inproj_kernel, first_shifted=first_shifted),
        grid=(n_tiles, m // tm),
        in_specs=[pl.BlockSpec((tm, d), lambda j, i: (i, 0)),
                  pl.BlockSpec((None, IN_TILE, d), lambda j, i: (l, j, 0)),
                  pl.BlockSpec((None, GLA_RANK, d), lambda j, i: (l, (j + 1) * rank_blocks, 0))],
        out_specs=pl.BlockSpec((None, tm, IN_TILE), lambda j, i: (j, i, 0)),
        out_shape=jax.ShapeDtypeStruct((n_tiles, m, IN_TILE), BF16),
        scratch_shapes=[pltpu.VMEM((IN_TILE, d), BF16)],
        compiler_params=_params("arbitrary", "arbitrary"),
        name="in_projection",
    )(h, w_in_t, w_in_t)


def _gla_tables(c):
    levels = int(np.log2(c))
    t = np.arange(c)
    tabs = [(t[None, :] <= t[:, None]),
            (t[None, :] > t[:, None])]
    for lv in range(1, levels + 1):
        size, half = 1 << lv, 1 << (lv - 1)
        mid = (t // size) * size + half
        second = t >= mid
        q_side = second[:, None] & (t[None, :] >= mid[:, None]) & (t[None, :] <= t[:, None])
        k_side = (~second)[:, None] & (t[None, :] > t[:, None]) & (t[None, :] < mid[:, None])
        tabs.append(q_side | k_side)
    e = np.concatenate(tabs, axis=0).astype(np.float32)
    x = t[:, None] ^ t[None, :]
    lvl = np.where(x == 0, 0, np.floor(np.log2(np.maximum(x, 1))).astype(np.int64) + 1)
    lvl = np.where(t[:, None] >= t[None, :], lvl, -1).astype(np.int32)
    return e, lvl, levels


def _gla_kernel(q_ref, k_ref, v_ref, r_ref, la_ref, gh_ref, e_ref, lvl_ref, o_ref, st_ref,
                *ex_refs, heads, dv, levels, n_chunks):
    c = GLA_CHUNK
    dk = GLA_DK

    @pl.when(pl.program_id(1) == 0)
    def _():
        st_ref[...] = jnp.zeros_like(st_ref)

    e_tab = e_ref[...]
    lvl = lvl_ref[...]
    row = lax.broadcasted_iota(jnp.int32, (c, 1), 0)
    gain = gh_ref[...]

    def chunk(ci, ex_ref):
        rows = pl.ds(pl.multiple_of(ci * c, c), c)
        g3 = jnp.concatenate(_split3(la_ref[rows, :]), axis=0)
        ex_ref[...] = jnp.exp(_dot(e_tab, g3))
        for h in range(heads):
            sl = slice(h * dk, (h + 1) * dk)
            vs = slice(h * dv, (h + 1) * dv)
            qh = q_ref[rows, sl].astype(F32) * (dk ** -0.5)
            kh = k_ref[rows, sl].astype(F32)
            vh = v_ref[rows, vs]
            st = st_ref[h]
            o = _dot_nt((qh * ex_ref[0:c, sl]).astype(BF16), st.astype(BF16))
            a = jnp.where(lvl == 0, _dot_nt(qh.astype(BF16), kh.astype(BF16)), 0.0)
            for lv in range(1, levels + 1):
                e_lv = ex_ref[(1 + lv) * c:(2 + lv) * c, sl]
                second = ((row >> (lv - 1)) & 1) == 1
                ql = jnp.where(second, qh * e_lv, 0.0).astype(BF16)
                kl = jnp.where(second, 0.0, kh * e_lv).astype(BF16)
                a = a + jnp.where(lvl == lv, _dot_nt(ql, kl), 0.0)
            o = o + _dot(a.astype(BF16), vh)
            kt = (kh * ex_ref[c:2 * c, sl]).astype(BF16)
            st_ref[h] = st * ex_ref[c - 1:c, sl] + _dot_tn(vh, kt)
            y = _rms(o, gain)
            r = r_ref[rows, vs].astype(F32)
            o_ref[rows, vs] = (y * (r * jax.nn.sigmoid(r))).astype(o_ref.dtype)

    def chunk_group(gi, carry):
        for u in range(GLA_CHUNKS_PER_TRIP):
            chunk(gi * GLA_CHUNKS_PER_TRIP + u, ex_refs[u])
        return carry

    lax.fori_loop(0, n_chunks // GLA_CHUNKS_PER_TRIP, chunk_group, 0)


def _gla(proj, log_a, g_head3, l, batch, seq, heads, dv):
    n_t, m, _ = proj.shape
    dk = GLA_DK
    rows = 512
    per_batch = seq // rows
    e_np, lvl_np, levels = _gla_tables(GLA_CHUNK)
    e_np = np.concatenate([e_np, e_np, e_np], axis=1)
    e_tab = jnp.asarray(e_np, BF16)
    lvl = jnp.asarray(lvl_np)
    kw = heads * dk
    assert 2 * kw == IN_TILE and heads * dv == IN_TILE
    kern = functools.partial(_gla_kernel, heads=heads, dv=dv, levels=levels,
                             n_chunks=rows // GLA_CHUNK)
    return pl.pallas_call(
        kern,
        grid=(batch, per_batch),
        in_specs=[pl.BlockSpec((None, rows, kw), lambda b, r: (0, b * per_batch + r, 0)),
                  pl.BlockSpec((None, rows, kw), lambda b, r: (0, b * per_batch + r, 1)),
                  pl.BlockSpec((None, rows, IN_TILE), lambda b, r: (1, b * per_batch + r, 0)),
                  pl.BlockSpec((None, rows, IN_TILE), lambda b, r: (2, b * per_batch + r, 0)),
                  pl.BlockSpec((rows, kw), lambda b, r: (b * per_batch + r, 0)),
                  pl.BlockSpec((None, 1, dv), lambda b, r: (l, 0, 0)),
                  pl.BlockSpec(e_np.shape, lambda b, r: (0, 0)),
                  pl.BlockSpec(lvl_np.shape, lambda b, r: (0, 0))],
        out_specs=pl.BlockSpec((rows, IN_TILE), lambda b, r: (b * per_batch + r, 0)),
        out_shape=jax.ShapeDtypeStruct((m, IN_TILE), BF16),
        scratch_shapes=[pltpu.VMEM((heads, dv, dk), F32)]
        + [pltpu.VMEM((e_np.shape[0], kw), F32)] * GLA_CHUNKS_PER_TRIP,
        compiler_params=_params("arbitrary", "arbitrary"),
        name="gla",
    )(proj, proj, proj, proj, log_a, g_head3, e_tab, lvl)


def _moba_kernel(q_ref, k_ref, v_ref, o_ref, kmean_ref, vt_ref, causal_ref, *tile_state,
                 n_blocks, topk, heads):
    blk, hd = MOBA_BLOCK, MOBA_HD
    i = pl.program_id(2)
    c2 = (hd ** -0.5) * LOG2E
    head_lanes = [slice(h * hd, (h + 1) * hd) for h in range(heads)]
    per_tile = len(tile_state) // 2
    even, odd = tile_state[:per_tile], tile_state[per_tile:]
    has_tile = i < n_blocks

    @pl.when(i == 0)
    def _():
        for h, lanes in enumerate(head_lanes):
            for n in range(n_blocks):
                rows = slice(n * blk, (n + 1) * blk)
                kmean_ref[h, n:n + 1, :] = jnp.mean(k_ref[rows, lanes].astype(F32), axis=0, keepdims=True)
                vt_ref[h, :, rows] = v_ref[rows, lanes].astype(F32).T.astype(BF16)

        kpos = lax.broadcasted_iota(jnp.int32, (blk, blk), 0)
        qpos = lax.broadcasted_iota(jnp.int32, (blk, blk), 1)
        causal_ref[...] = jnp.where(kpos <= qpos, 0.0, MASK_VALUE)
        for _, mblk_ref, _, _, l_ref, acc_ref in (even, odd):
            mblk_ref[...] = jnp.zeros_like(mblk_ref)
            l_ref[...] = jnp.ones_like(l_ref)
            acc_ref[...] = jnp.zeros_like(acc_ref)

    def pairwise(n, body):
        def two(t, carry):
            body(2 * t)
            body(2 * t + 1)
            return carry

        lax.fori_loop(0, lax.shift_right_logical(n, 1), two, 0)

        @pl.when(n % 2 == 1)
        def _():
            body(n - 1)

    def step(this, last):
        bias_ref, mblk_ref, s_ref, m_ref, l_ref, acc_ref = this
        bias_last, _, s_last, m_last, l_last, acc_last = last

        @pl.when(has_tile)
        def _():
            for h, lanes in enumerate(head_lanes):
                q = q_ref[:, lanes]
                m1, m2, m3 = _split3(kmean_ref[h])
                s_blk = _dot_nt(m1, q) + _dot_nt(m2, q) + _dot_nt(m3, q)
                bidx = lax.broadcasted_iota(jnp.int32, s_blk.shape, 0)
                cur = jnp.where(bidx < i, s_blk, MASK_VALUE)
                sel = jnp.zeros(s_blk.shape, jnp.bool_)
                for _ in range(topk):
                    mx = jnp.max(cur, axis=0, keepdims=True)
                    idx = jnp.min(jnp.where(cur == mx, bidx, n_blocks), axis=0, keepdims=True)
                    pick = bidx == idx
                    sel = jnp.logical_or(sel, jnp.logical_and(pick, idx < i))
                    cur = jnp.where(pick, -jnp.inf, cur)
                bias_ref[h] = jnp.where(jnp.logical_or(sel, bidx == i), 0.0, MASK_VALUE)

        def score(j, causal):
            cols = pl.ds(pl.multiple_of(j * blk, blk), blk)
            for h, lanes in enumerate(head_lanes):
                s = _dot_nt(k_ref[cols, lanes], q_ref[:, lanes]) * c2
                if causal:
                    s = s + causal_ref[...]
                s_ref[h, j] = s
                mblk_ref[h, pl.ds(j, 1), :] = jnp.max(s, axis=0, keepdims=True)

        def attend(j):
            cols = pl.ds(pl.multiple_of(j * blk, blk), blk)
            for h in range(heads):
                p = jnp.exp2(s_last[h, j] + (bias_last[h, pl.ds(j, 1), :] - m_last[h]))
                l_last[h] += jnp.sum(p, axis=0, keepdims=True)
                acc_last[h] += _dot(vt_ref[h, :, cols], p.astype(BF16))

        def both(j):
            score(j, False)
            attend(j)

        @pl.when(has_tile)
        def _():
            pairwise(i, both)
            score(i, True)
            for h in range(heads):
                bidx = lax.broadcasted_iota(jnp.int32, (n_blocks, blk), 0)
                allowed_max = jnp.where(bidx <= i, mblk_ref[h] + bias_ref[h], -jnp.inf)
                m_ref[h] = jnp.max(allowed_max, axis=0, keepdims=True)
                l_ref[h] = jnp.zeros((1, blk), F32)
                acc_ref[h] = jnp.zeros((hd, blk), F32)

        @pl.when(jnp.logical_not(has_tile))
        def _():
            pairwise(i, attend)

        for h, lanes in enumerate(head_lanes):
            o_ref[:, lanes] = (acc_last[h] / l_last[h]).T.astype(o_ref.dtype)

    @pl.when(i % 2 == 0)
    def _():
        step(even, odd)

    @pl.when(i % 2 == 1)
    def _():
        step(odd, even)


def _moba(proj, batch, seq, heads):
    n_t, m, _ = proj.shape
    hd, blk = MOBA_HD, MOBA_BLOCK
    hp = MOBA_HEADS_PER_STEP
    assert seq % blk == 0 and heads * hd == IN_TILE and heads % hp == 0
    n_blocks = seq // blk
    topk = min(MOBA_TOPK, n_blocks - 1)
    wide = hp * hd
    kern = functools.partial(_moba_kernel, n_blocks=n_blocks, topk=topk, heads=hp)
    tile_state = [pltpu.VMEM((hp, n_blocks, blk), F32),
                  pltpu.VMEM((hp, n_blocks, blk), F32),
                  pltpu.VMEM((hp, n_blocks, blk, blk), F32),
                  pltpu.VMEM((hp, 1, blk), F32),
                  pltpu.VMEM((hp, 1, blk), F32),
                  pltpu.VMEM((hp, hd, blk), F32)]
    q_tile = lambda i: jnp.minimum(i, n_blocks - 1)
    o_tile = lambda i: jnp.maximum(i - 1, 0)
    return pl.pallas_call(
        kern,
        grid=(batch, heads // hp, n_blocks + 1),
        in_specs=[pl.BlockSpec((None, blk, wide), lambda b, g, i: (3, b * n_blocks + q_tile(i), g)),
                  pl.BlockSpec((None, seq, wide), lambda b, g, i: (4, b, g)),
                  pl.BlockSpec((None, seq, wide), lambda b, g, i: (5, b, g))],
        out_specs=pl.BlockSpec((blk, wide), lambda b, g, i: (b * n_blocks + o_tile(i), g)),
        out_shape=jax.ShapeDtypeStruct((m, IN_TILE), BF16),
        scratch_shapes=[pltpu.VMEM((hp, n_blocks, hd), F32),
                        pltpu.VMEM((hp, hd, seq), BF16),
                        pltpu.VMEM((blk, blk), F32)] + tile_state + tile_state,
        compiler_params=_params("arbitrary", "arbitrary", "arbitrary"),
        name="moba",
    )(proj, proj, proj)


def _merge_kernel(og_ref, om_ref, wg_ref, wm_ref, ga_ref, gb_ref, o_ref, wgb_ref, wmb_ref):
    @pl.when(pl.program_id(1) == 0)
    def _():
        wgb_ref[...] = wg_ref[...].astype(BF16)
        wmb_ref[...] = wm_ref[...].astype(BF16)

    ya = _dot(og_ref[...], wgb_ref[...])
    yb = _dot(om_ref[...], wmb_ref[...])
    ga = jax.nn.sigmoid(ga_ref[...].astype(F32))
    gb = jax.nn.sigmoid(gb_ref[...].astype(F32))
    o_ref[...] = (ga * ya + gb * yb).astype(o_ref.dtype)


def _merge(o_gla, o_moba, w_gla_out, w_moba_out, proj, l, d, gate_a0):
    m, kg = o_gla.shape
    km = o_moba.shape[1]
    tm, tn = 1024, IN_TILE
    n_tiles = d // tn
    gate_b0 = gate_a0 + n_tiles
    return pl.pallas_call(
        _merge_kernel,
        grid=(n_tiles, m // tm),
        in_specs=[pl.BlockSpec((tm, kg), lambda j, i: (i, 0)),
                  pl.BlockSpec((tm, km), lambda j, i: (i, 0)),
                  pl.BlockSpec((None, kg, tn), lambda j, i: (l, 0, j)),
                  pl.BlockSpec((None, km, tn), lambda j, i: (l, 0, j)),
                  pl.BlockSpec((None, tm, tn), lambda j, i: (gate_a0 + j, i, 0)),
                  pl.BlockSpec((None, tm, tn), lambda j, i: (gate_b0 + j, i, 0))],
        out_specs=pl.BlockSpec((tm, tn), lambda j, i: (i, j)),
        out_shape=jax.ShapeDtypeStruct((m, d), BF16),
        scratch_shapes=[pltpu.VMEM((kg, tn), BF16), pltpu.VMEM((km, tn), BF16)],
        compiler_params=_params("arbitrary", "arbitrary"),
        name="branch_merge",
    )(o_gla, o_moba, w_gla_out, w_moba_out, proj, proj)


def _matmul_kernel(a_ref, w_ref, o_ref, wbf_ref):
    @pl.when(pl.program_id(1) == 0)
    def _():
        wbf_ref[...] = w_ref[...].astype(BF16)

    o_ref[...] = _dot(a_ref[...], wbf_ref[...]).astype(o_ref.dtype)


def _matmul(a, w, l, tm, tn, name):
    m, k = a.shape
    n = w.shape[-1]
    return pl.pallas_call(
        _matmul_kernel,
        grid=(n // tn, m // tm),
        in_specs=[pl.BlockSpec((tm, k), lambda j, i: (i, 0)),
                  pl.BlockSpec((None, k, tn), lambda j, i: (l, 0, j))],
        out_specs=pl.BlockSpec((tm, tn), lambda j, i: (i, j)),
        out_shape=jax.ShapeDtypeStruct((m, n), BF16),
        scratch_shapes=[pltpu.VMEM((k, tn), BF16)],
        compiler_params=_params("arbitrary", "arbitrary"),
        name=name,
    )(a, w)


def _ffn_row_permutation(tile):
    g = tile // V7X_SUBLANES
    pos = np.arange(tile)
    time = (pos % V7X_SUBLANES) * g + pos // V7X_SUBLANES
    perm = np.zeros((tile, tile), np.float32)
    perm[pos, time] = 1.0
    return perm


def _post_mix_kernel(x_ref, y_ref, gpost_ref, mod_ref, gpre_ref, perm_ref, o_ref, h_ref,
                     *, gate_row, shift_row, scale_row):
    y = y_ref[...].astype(F32)
    x = x_ref[...] + mod_ref[gate_row:gate_row + 1, :] * _rms(y, gpost_ref[...])
    o_ref[...] = x
    h = _rms(x, gpre_ref[...])
    h = h * (1.0 + mod_ref[scale_row:scale_row + 1, :]) + mod_ref[shift_row:shift_row + 1, :]
    h_ref[...] = _dot(perm_ref[...], h.astype(BF16)).astype(h_ref.dtype)


def _post_mix(x2, y, gpost3, mod3, gpre3, l, seq):
    m, d = x2.shape
    tm = FFN_TILE
    per_batch = seq // tm
    row_spec = pl.BlockSpec((tm, d), lambda i: (i, 0))
    gain_spec = pl.BlockSpec((None, 1, d), lambda i: (l, 0, 0))
    perm = jnp.asarray(_ffn_row_permutation(tm), BF16)
    return pl.pallas_call(
        functools.partial(_post_mix_kernel, gate_row=2, shift_row=3, scale_row=4),
        grid=(m // tm,),
        in_specs=[row_spec, row_spec, gain_spec,
                  pl.BlockSpec((None, N_MOD, d), lambda i: (i // per_batch, 0, 0)),
                  gain_spec,
                  pl.BlockSpec((tm, tm), lambda i: (0, 0))],
        out_specs=(row_spec, row_spec),
        out_shape=(jax.ShapeDtypeStruct((m, d), F32), jax.ShapeDtypeStruct((m, d), BF16)),
        compiler_params=_params("arbitrary"),
        name="post_mix_pre_ffn",
    )(x2, y, gpost3, mod3, gpre3, perm)


def _post_ffn_kernel(x_ref, y_ref, gpost_ref, mod_ref, unperm_ref, o_ref, *, gate_row):
    y = _dot(unperm_ref[...], y_ref[...])
    o_ref[...] = x_ref[...] + mod_ref[gate_row:gate_row + 1, :] * _rms(y, gpost_ref[...])


def _post_ffn(x2, y, gpost3, mod3, l, seq):
    m, d = x2.shape
    tm = FFN_TILE
    per_batch = seq // tm
    row_spec = pl.BlockSpec((tm, d), lambda i: (i, 0))
    unperm = jnp.asarray(_ffn_row_permutation(tm).T, BF16)
    return pl.pallas_call(
        functools.partial(_post_ffn_kernel, gate_row=5),
        grid=(m // tm,),
        in_specs=[row_spec, row_spec,
                  pl.BlockSpec((None, 1, d), lambda i: (l, 0, 0)),
                  pl.BlockSpec((None, N_MOD, d), lambda i: (i // per_batch, 0, 0)),
                  pl.BlockSpec((tm, tm), lambda i: (0, 0))],
        out_specs=row_spec,
        out_shape=jax.ShapeDtypeStruct((m, d), F32),
        compiler_params=_params("arbitrary"),
        name="post_ffn",
    )(x2, y, gpost3, mod3, unperm)


def _ffn_up_kernel(h_ref, wu_ref, wg_ref, cu_ref, cg_ref, bu_ref, bg_ref, o_ref,
                   wbf_ref, up_even_ref, up_odd_ref, *, row_tiles, tiles_per_batch, n_pairs):
    s = pl.program_id(0)
    tm, tn = o_ref.shape
    head = 2 * V7X_SUBLANES
    g_lane0 = tn + FFN_PAD_LANES
    matmul_starts_batch = (s % row_tiles) % tiles_per_batch == 0

    @pl.when(s == 0)
    def _():
        up_even_ref[...] = jnp.zeros_like(up_even_ref)
        up_odd_ref[...] = jnp.zeros_like(up_odd_ref)

    @pl.when(jnp.logical_and(s % row_tiles == 0, s < n_pairs))
    def _():
        wbf_ref[:, :tn] = wu_ref[...].astype(BF16)
        wbf_ref[:, tn:] = wg_ref[...].astype(BF16)

    def step(cur_ref, prev_ref):
        sub = V7X_SUBLANES

        def conv(r0, half, w_ref, b_ref):
            cols = slice(half * g_lane0, half * g_lane0 + tn)
            acc = b_ref[...]
            for tap in range(CONV_WIDTH):
                start = pl.multiple_of(r0 + tap * sub, sub)
                acc = acc + w_ref[tap:tap + 1, :] * prev_ref[pl.ds(start, FFN_ROW_GROUP), cols]
            return acc

        def chunk(c, carry):
            r0 = pl.multiple_of(c * FFN_ROW_CHUNK, FFN_ROW_CHUNK)
            up = _dot(h_ref[pl.ds(r0, FFN_ROW_CHUNK), :], wbf_ref[...])
            cur_ref[pl.ds(r0 + head, FFN_ROW_CHUNK), 0:tn] = up[:, :tn]
            cur_ref[pl.ds(r0 + head, FFN_ROW_CHUNK), g_lane0:g_lane0 + tn] = up[:, tn:]
            for r in range(0, FFN_ROW_CHUNK, FFN_ROW_GROUP):
                rg = pl.multiple_of(r0 + r, FFN_ROW_GROUP)
                u = conv(rg, 0, cu_ref, bu_ref)
                g = conv(rg, 1, cg_ref, bg_ref)
                gelu = 0.5 * g * (1.0 + jnp.tanh(np.sqrt(2.0 / np.pi) * (g + 0.044715 * (g * g * g))))
                o_ref[pl.ds(rg, FFN_ROW_GROUP), :] = (gelu * u).astype(o_ref.dtype)
            return carry

        lax.fori_loop(0, tm // FFN_ROW_CHUNK, chunk, 0)

        own = pltpu.roll(cur_ref[tm:tm + head, :], 1, axis=0)
        before = jnp.where(matmul_starts_batch, 0.0,
                           pltpu.roll(prev_ref[tm:tm + head, :], 1 + V7X_SUBLANES, axis=0))
        first = lax.broadcasted_iota(jnp.int32, (head, 1), 0) % V7X_SUBLANES == 0
        cur_ref[0:head, :] = jnp.where(first, before, own)

    @pl.when(s % 2 == 0)
    def _():
        step(up_even_ref, up_odd_ref)

    @pl.when(s % 2 == 1)
    def _():
        step(up_odd_ref, up_even_ref)


def _ffn_up(h, w_up, w_conv, b_conv3, l, seq):
    m, d = h.shape
    f = w_up.shape[-1] // 2
    tm, tn = FFN_TILE, 512
    nt = f // tn
    row_tiles = m // tm
    n_pairs = nt * row_tiles
    kern = functools.partial(_ffn_up_kernel, row_tiles=row_tiles, tiles_per_batch=seq // tm,
                             n_pairs=n_pairs)
    mm = lambda s: jnp.minimum(s, n_pairs - 1)
    ep = lambda s: jnp.maximum(s - 1, 0)
    return pl.pallas_call(
        kern,
        grid=(n_pairs + 1,),
        in_specs=[pl.BlockSpec((tm, d), lambda s: (mm(s) % row_tiles, 0)),
                  pl.BlockSpec((None, d, tn), lambda s: (l, 0, mm(s) // row_tiles)),
                  pl.BlockSpec((None, d, tn), lambda s: (l, 0, nt + mm(s) // row_tiles)),
                  pl.BlockSpec((None, CONV_WIDTH, tn), lambda s: (l, 0, ep(s) // row_tiles)),
                  pl.BlockSpec((None, CONV_WIDTH, tn), lambda s: (l, 0, nt + ep(s) // row_tiles)),
                  pl.BlockSpec((None, 1, tn), lambda s: (l, 0, ep(s) // row_tiles)),
                  pl.BlockSpec((None, 1, tn), lambda s: (l, 0, nt + ep(s) // row_tiles))],
        out_specs=pl.BlockSpec((tm, tn), lambda s: (ep(s) % row_tiles, ep(s) // row_tiles)),
        out_shape=jax.ShapeDtypeStruct((m, f), BF16),
        scratch_shapes=[pltpu.VMEM((d, 2 * tn), BF16),
                        pltpu.VMEM((tm + 2 * V7X_SUBLANES, 2 * tn + FFN_PAD_LANES), F32),
                        pltpu.VMEM((tm + 2 * V7X_SUBLANES, 2 * tn + FFN_PAD_LANES), F32)],
        compiler_params=_params("arbitrary"),
        name="ffn_up_conv_gate",
    )(h, w_up, w_up, w_conv, w_conv, b_conv3, b_conv3)


def kernel(x, c, w_ada, b_ada, g_pre_mix, w_in, w_alpha_up, b_alpha, g_gla_head, w_gla_out,
           w_moba_out, w_o, g_post_mix, g_pre_ffn, w_up, w_conv, b_conv, w_down, g_post_ffn):
    batch, seq, d = x.shape
    depth = w_ada.shape[0]
    m = batch * seq
    gla_k = w_alpha_up.shape[-1]
    gla_heads = gla_k // GLA_DK
    gla_dv = g_gla_head.shape[-1]
    gla_v = w_gla_out.shape[1]
    moba_w = w_moba_out.shape[1]
    moba_heads = moba_w // MOBA_HD
    low_col = 2 * gla_k + 2 * gla_v
    assert gla_v == gla_heads * gla_dv and d % IN_TILE == 0
    assert low_col % IN_TILE == 0 and w_in.shape[-1] == low_col + GLA_RANK + 3 * moba_w + 2 * d
    first_shifted = low_col // IN_TILE
    n_tiles = first_shifted + 3 * moba_w // IN_TILE + 2 * d // IN_TILE

    x2 = x.reshape(m, d)
    c8 = jnp.pad(c, ((0, 8 - batch), (0, 0)))
    as3 = lambda a: a.reshape(a.shape[0], 1, a.shape[-1])
    wup_pad = jnp.pad(w_alpha_up, ((0, 0), (0, V7X_LANES - GLA_RANK), (0, 0)))
    w_in_t = jnp.swapaxes(w_in, 1, 2)

    for l in range(depth):
        mod = _modulation(c8, w_ada, as3(b_ada), l)
        mod3 = mod[:batch].reshape(batch, N_MOD, d)
        h, log_a = _prenorm(x2, as3(g_pre_mix), mod3, w_in_t, wup_pad, as3(b_alpha), l, seq,
                            shift_row=0, scale_row=1, low_block=low_col // V7X_LANES)
        proj = _in_projection(h, w_in_t, l, n_tiles, first_shifted)
        o_gla = _gla(proj, log_a, as3(g_gla_head), l, batch, seq, gla_heads, gla_dv)
        o_moba = _moba(proj, batch, seq, moba_heads)
        merged = _merge(o_gla, o_moba, w_gla_out, w_moba_out, proj, l, d,
                        gate_a0=first_shifted + 3 * moba_w // IN_TILE)
        y = _matmul(merged, w_o, l, 1024, 1024, "out_projection")
        x2, h2 = _post_mix(x2, y, as3(g_post_mix), mod3, as3(g_pre_ffn), l, seq)
        act = _ffn_up(h2, w_up, w_conv, as3(b_conv), l, seq)
        y2 = _matmul(act, w_down, l, 512, 512, "ffn_down")
        x2 = _post_ffn(x2, y2, as3(g_post_ffn), mod3, l, seq)
    return x2.reshape(batch, seq, d)
```

```python
import functools

import numpy as np
import jax
import jax.numpy as jnp
from jax import lax
from jax.experimental import pallas as pl
from jax.experimental.pallas import tpu as pltpu

F32 = jnp.float32
BF16 = jnp.bfloat16

GLA_DK = 128
GLA_RANK = 16
GLA_GATE_TEMP = 16.0
GLA_CHUNK = 64
GLA_CHUNKS_PER_TRIP = 4
MOBA_HD = 128
MOBA_BLOCK = 256
MOBA_TOPK = 3
MOBA_HEADS_PER_STEP = 2
CONV_WIDTH = 3
N_MOD = 6
EPS = 1e-6

V7X_LANES = 128
V7X_SUBLANES = 8
V7X_VMEM_LIMIT_BYTES = 56 * 1024 * 1024

IN_TILE = 1024
MASK_VALUE = -1e30
LOG2E = 1.4426950408889634
FFN_TILE = 512
FFN_ROW_CHUNK = 256
FFN_ROW_GROUP = 16
FFN_PAD_LANES = 128


def _params(*semantics):
    return pltpu.CompilerParams(dimension_semantics=semantics,
                                vmem_limit_bytes=V7X_VMEM_LIMIT_BYTES)


def _dot(a, b):
    return jnp.dot(a, b, preferred_element_type=F32)


def _dot_nt(a, b):
    return lax.dot_general(a, b, (((1,), (1,)), ((), ())), preferred_element_type=F32)


def _dot_tn(a, b):
    return lax.dot_general(a, b, (((0,), (0,)), ((), ())), preferred_element_type=F32)


def _split3(x):
    x1 = x.astype(BF16)
    r1 = x - x1.astype(F32)
    x2 = r1.astype(BF16)
    x3 = (r1 - x2.astype(F32)).astype(BF16)
    return x1, x2, x3


def _rms(x, gain):
    return x * lax.rsqrt(jnp.mean(x * x, axis=-1, keepdims=True) + EPS) * gain


def _mod_kernel(c_ref, w_ref, b_ref, o_ref):
    c = c_ref[...]
    s = c * jax.nn.sigmoid(c)
    o_ref[...] = _dot(s.astype(BF16), w_ref[...].astype(BF16)) + b_ref[...]


def _modulation(c8, w_ada, b_ada3, l):
    rows, d = c8.shape
    n = w_ada.shape[-1]
    tn = 1024
    return pl.pallas_call(
        _mod_kernel,
        grid=(n // tn,),
        in_specs=[pl.BlockSpec((rows, d), lambda j: (0, 0)),
                  pl.BlockSpec((None, d, tn), lambda j: (l, 0, j)),
                  pl.BlockSpec((None, 1, tn), lambda j: (l, 0, j))],
        out_specs=pl.BlockSpec((rows, tn), lambda j: (0, j)),
        out_shape=jax.ShapeDtypeStruct((rows, n), F32),
        compiler_params=_params("arbitrary"),
        name="adaln_mod",
    )(c8, w_ada, b_ada3)


def _prenorm_kernel(x_ref, g_ref, mod_ref, w_ref, wup_ref, b_ref, o_ref, la_ref,
                    *, shift_row, scale_row):
    y = _rms(x_ref[...], g_ref[...])
    sh = mod_ref[shift_row:shift_row + 1, :]
    sc = mod_ref[scale_row:scale_row + 1, :]
    h = (y * (1.0 + sc) + sh).astype(o_ref.dtype)
    o_ref[...] = h
    low = _dot_nt(h, w_ref[...].astype(BF16))
    z = _dot(low.astype(BF16), wup_ref[...].astype(BF16)) + b_ref[...]
    log_sig = jnp.minimum(z, 0.0) - jnp.log1p(jnp.exp(-jnp.abs(z)))
    la_ref[...] = log_sig / GLA_GATE_TEMP


def _prenorm(x2, gain3, mod3, w_in_t, wup_pad, b_alpha3, l, seq, shift_row, scale_row, low_block):
    m, d = x2.shape
    k = wup_pad.shape[-1]
    tm = 512
    per_batch = seq // tm
    return pl.pallas_call(
        functools.partial(_prenorm_kernel, shift_row=shift_row, scale_row=scale_row),
        grid=(m // tm,),
        in_specs=[pl.BlockSpec((tm, d), lambda i: (i, 0)),
                  pl.BlockSpec((None, 1, d), lambda i: (l, 0, 0)),
                  pl.BlockSpec((None, N_MOD, d), lambda i: (i // per_batch, 0, 0)),
                  pl.BlockSpec((None, V7X_LANES, d), lambda i: (l, low_block, 0)),
                  pl.BlockSpec((None, V7X_LANES, k), lambda i: (l, 0, 0)),
                  pl.BlockSpec((None, 1, k), lambda i: (l, 0, 0))],
        out_specs=(pl.BlockSpec((tm, d), lambda i: (i, 0)),
                   pl.BlockSpec((tm, k), lambda i: (i, 0))),
        out_shape=(jax.ShapeDtypeStruct((m, d), BF16), jax.ShapeDtypeStruct((m, k), F32)),
        compiler_params=_params("arbitrary"),
        name="prenorm_mix",
    )(x2, gain3, mod3, w_in_t, wup_pad, b_alpha3)


def _inproj_kernel(h_ref, wa_ref, wb_ref, o_ref, wbf_ref, *, first_shifted):
    j = pl.program_id(0)
    i = pl.program_id(1)
    keep = IN_TILE - GLA_RANK

    @pl.when(jnp.logical_and(i == 0, j < first_shifted))
    def _():
        wbf_ref[...] = wa_ref[...].astype(BF16)

    @pl.when(jnp.logical_and(i == 0, j >= first_shifted))
    def _():
        wbf_ref[:keep, :] = wa_ref[GLA_RANK:, :].astype(BF16)
        wbf_ref[keep:, :] = wb_ref[...].astype(BF16)

    o_ref[...] = _dot_nt(h_ref[...], wbf_ref[...]).astype(o_ref.dtype)


def _in_projection(h, w_in_t, l, n_tiles, first_shifted):
    m, d = h.shape
    tm = 1024
    rank_blocks = IN_TILE // GLA_RANK
    return pl.pallas_call(
        functools.partial(_inproj_kernel, first_shifted=first_shifted),
        grid=(n_tiles, m // tm),
        in_specs=[pl.BlockSpec((tm, d), lambda j, i: (i, 0)),
                  pl.BlockSpec((None, IN_TILE, d), lambda j, i: (l, j, 0)),
                  pl.BlockSpec((None, GLA_RANK, d), lambda j, i: (l, (j + 1) * rank_blocks, 0))],
        out_specs=pl.BlockSpec((None, tm, IN_TILE), lambda j, i: (j, i, 0)),
        out_shape=jax.ShapeDtypeStruct((n_tiles, m, IN_TILE), BF16),
        scratch_shapes=[pltpu.VMEM((IN_TILE, d), BF16)],
        compiler_params=_params("arbitrary", "arbitrary"),
        name="in_projection",
    )(h, w_in_t, w_in_t)


def _gla_tables(c):
    levels = int(np.log2(c))
    t = np.arange(c)
    tabs = [(t[None, :] <= t[:, None]),
            (t[None, :] > t[:, None])]
    for lv in range(1, levels + 1):
        size, half = 1 << lv, 1 << (lv - 1)
        mid = (t // size) * size + half
        second = t >= mid
        q_side = second[:, None] & (t[None, :] >= mid[:, None]) & (t[None, :] <= t[:, None])
        k_side = (~second)[:, None] & (t[None, :] > t[:, None]) & (t[None, :] < mid[:, None])
        tabs.append(q_side | k_side)
    e = np.concatenate(tabs, axis=0).astype(np.float32)
    x = t[:, None] ^ t[None, :]
    lvl = np.where(x == 0, 0, np.floor(np.log2(np.maximum(x, 1))).astype(np.int64) + 1)
    lvl = np.where(t[:, None] >= t[None, :], lvl, -1).astype(np.int32)
    return e, lvl, levels


def _gla_kernel(q_ref, k_ref, v_ref, r_ref, la_ref, gh_ref, e_ref, lvl_ref, o_ref, st_ref,
                *ex_refs, heads, dv, levels, n_chunks):
    c = GLA_CHUNK
    dk = GLA_DK

    @pl.when(pl.program_id(1) == 0)
    def _():
        st_ref[...] = jnp.zeros_like(st_ref)

    e_tab = e_ref[...]
    lvl = lvl_ref[...]
    row = lax.broadcasted_iota(jnp.int32, (c, 1), 0)
    gain = gh_ref[...]

    def chunk(ci, ex_ref):
        rows = pl.ds(pl.multiple_of(ci * c, c), c)
        g3 = jnp.concatenate(_split3(la_ref[rows, :]), axis=0)
        ex_ref[...] = jnp.exp(_dot(e_tab, g3))
        for h in range(heads):
            sl = slice(h * dk, (h + 1) * dk)
            vs = slice(h * dv, (h + 1) * dv)
            qh = q_ref[rows, sl].astype(F32) * (dk ** -0.5)
            kh = k_ref[rows, sl].astype(F32)
            vh = v_ref[rows, vs]
            st = st_ref[h]
            o = _dot_nt((qh * ex_ref[0:c, sl]).astype(BF16), st.astype(BF16))
            a = jnp.where(lvl == 0, _dot_nt(qh.astype(BF16), kh.astype(BF16)), 0.0)
            for lv in range(1, levels + 1):
                e_lv = ex_ref[(1 + lv) * c:(2 + lv) * c, sl]
                second = ((row >> (lv - 1)) & 1) == 1
                ql = jnp.where(second, qh * e_lv, 0.0).astype(BF16)
                kl = jnp.where(second, 0.0, kh * e_lv).astype(BF16)
                a = a + jnp.where(lvl == lv, _dot_nt(ql, kl), 0.0)
            o = o + _dot(a.astype(BF16), vh)
            kt = (kh * ex_ref[c:2 * c, sl]).astype(BF16)
            st_ref[h] = st * ex_ref[c - 1:c, sl] + _dot_tn(vh, kt)
            y = _rms(o, gain)
            r = r_ref[rows, vs].astype(F32)
            o_ref[rows, vs] = (y * (r * jax.nn.sigmoid(r))).astype(o_ref.dtype)

    def chunk_group(gi, carry):
        for u in range(GLA_CHUNKS_PER_TRIP):
            chunk(gi * GLA_CHUNKS_PER_TRIP + u, ex_refs[u])
        return carry

    lax.fori_loop(0, n_chunks // GLA_CHUNKS_PER_TRIP, chunk_group, 0)


def _gla(proj, log_a, g_head3, l, batch, seq, heads, dv):
    n_t, m, _ = proj.shape
    dk = GLA_DK
    rows = 512
    per_batch = seq // rows
    e_np, lvl_np, levels = _gla_tables(GLA_CHUNK)
    e_np = np.concatenate([e_np, e_np, e_np], axis=1)
    e_tab = jnp.asarray(e_np, BF16)
    lvl = jnp.asarray(lvl_np)
    kw = heads * dk
    assert 2 * kw == IN_TILE and heads * dv == IN_TILE
    kern = functools.partial(_gla_kernel, heads=heads, dv=dv, levels=levels,
                             n_chunks=rows // GLA_CHUNK)
    return pl.pallas_call(
        kern,
        grid=(batch, per_batch),
        in_specs=[pl.BlockSpec((None, rows, kw), lambda b, r: (0, b * per_batch + r, 0)),
                  pl.BlockSpec((None, rows, kw), lambda b, r: (0, b * per_batch + r, 1)),
                  pl.BlockSpec((None, rows, IN_TILE), lambda b, r: (1, b * per_batch + r, 0)),
                  pl.BlockSpec((None, rows, IN_TILE), lambda b, r: (2, b * per_batch + r, 0)),
                  pl.BlockSpec((rows, kw), lambda b, r: (b * per_batch + r, 0)),
                  pl.BlockSpec((None, 1, dv), lambda b, r: (l, 0, 0)),
                  pl.BlockSpec(e_np.shape, lambda b, r: (0, 0)),
                  pl.BlockSpec(lvl_np.shape, lambda b, r: (0, 0))],
        out_specs=pl.BlockSpec((rows, IN_TILE), lambda b, r: (b * per_batch + r, 0)),
        out_shape=jax.ShapeDtypeStruct((m, IN_TILE), BF16),
        scratch_shapes=[pltpu.VMEM((heads, dv, dk), F32)]
        + [pltpu.VMEM((e_np.shape[0], kw), F32)] * GLA_CHUNKS_PER_TRIP,
        compiler_params=_params("arbitrary", "arbitrary"),
        name="gla",
    )(proj, proj, proj, proj, log_a, g_head3, e_tab, lvl)


def _moba_kernel(q_ref, k_ref, v_ref, qall_ref, o_ref, kmean_ref, vt_ref, causal_ref, bias_ref,
                 *tile_state,
                 n_blocks, topk, heads):
    blk, hd = MOBA_BLOCK, MOBA_HD
    i = pl.program_id(2)
    c2 = (hd ** -0.5) * LOG2E
    head_lanes = [slice(h * hd, (h + 1) * hd) for h in range(heads)]
    per_tile = len(tile_state) // 2
    even, odd = tile_state[:per_tile], tile_state[per_tile:]
    has_tile = i < n_blocks

    @pl.when(i == 0)
    def _():
        for h, lanes in enumerate(head_lanes):
            for n in range(n_blocks):
                rows = slice(n * blk, (n + 1) * blk)
                kmean_ref[h, n:n + 1, :] = jnp.mean(k_ref[rows, lanes].astype(F32), axis=0, keepdims=True)
                vt_ref[h, :, rows] = v_ref[rows, lanes].astype(F32).T.astype(BF16)

        kpos = lax.broadcasted_iota(jnp.int32, (blk, blk), 0)
        qpos = lax.broadcasted_iota(jnp.int32, (blk, blk), 1)
        causal_ref[...] = jnp.where(kpos <= qpos, 0.0, MASK_VALUE)
        for mblk_ref, _, _, l_ref, acc_ref in (even, odd):
            mblk_ref[...] = jnp.zeros_like(mblk_ref)
            l_ref[...] = jnp.ones_like(l_ref)
            acc_ref[...] = jnp.zeros_like(acc_ref)

        for h, lanes in enumerate(head_lanes):
            m1, m2, m3 = _split3(kmean_ref[h])
            for t in range(n_blocks):
                q = qall_ref[t * blk:(t + 1) * blk, lanes]
                s_blk = _dot_nt(m1, q) + _dot_nt(m2, q) + _dot_nt(m3, q)
                bidx = lax.broadcasted_iota(jnp.int32, s_blk.shape, 0)
                cur = jnp.where(bidx < t, s_blk, MASK_VALUE)
                sel = jnp.zeros(s_blk.shape, jnp.bool_)
                for _ in range(topk):
                    mx = jnp.max(cur, axis=0, keepdims=True)
                    idx = jnp.min(jnp.where(cur == mx, bidx, n_blocks), axis=0, keepdims=True)
                    pick = bidx == idx
                    sel = jnp.logical_or(sel, jnp.logical_and(pick, idx < t))
                    cur = jnp.where(pick, -jnp.inf, cur)
                bias_ref[h, t] = jnp.where(jnp.logical_or(sel, bidx == t), 0.0, MASK_VALUE)

    def pairwise(n, body):
        def two(t, carry):
            body(2 * t)
            body(2 * t + 1)
            return carry

        lax.fori_loop(0, lax.shift_right_logical(n, 1), two, 0)

        @pl.when(n % 2 == 1)
        def _():
            body(n - 1)

    def step(this, last):
        mblk_ref, s_ref, m_ref, l_ref, acc_ref = this
        _, s_last, m_last, l_last, acc_last = last
        i_last = jnp.maximum(i - 1, 0)

        def score(j, causal):
            cols = pl.ds(pl.multiple_of(j * blk, blk), blk)
            for h, lanes in enumerate(head_lanes):
                s = _dot_nt(k_ref[cols, lanes], q_ref[:, lanes]) * c2
                if causal:
                    s = s + causal_ref[...]
                s_ref[h, j] = s
                mblk_ref[h, pl.ds(j, 1), :] = jnp.max(s, axis=0, keepdims=True)

        def attend(j):
            cols = pl.ds(pl.multiple_of(j * blk, blk), blk)
            for h in range(heads):
                p = jnp.exp2(s_last[h, j] + (bias_ref[h, i_last, pl.ds(j, 1), :] - m_last[h]))
                l_last[h] += jnp.sum(p, axis=0, keepdims=True)
                acc_last[h] += _dot(vt_ref[h, :, cols], p.astype(BF16))

        def both(j):
            score(j, False)
            attend(j)

        @pl.when(has_tile)
        def _():
            pairwise(i, both)
            score(i, True)
            for h in range(heads):
                bidx = lax.broadcasted_iota(jnp.int32, (n_blocks, blk), 0)
                allowed_max = jnp.where(bidx <= i, mblk_ref[h] + bias_ref[h, i], -jnp.inf)
                m_ref[h] = jnp.max(allowed_max, axis=0, keepdims=True)
                l_ref[h] = jnp.zeros((1, blk), F32)
                acc_ref[h] = jnp.zeros((hd, blk), F32)

        @pl.when(jnp.logical_not(has_tile))
        def _():
            pairwise(i, attend)

        for h, lanes in enumerate(head_lanes):
            o_ref[:, lanes] = (acc_last[h] / l_last[h]).T.astype(o_ref.dtype)

    @pl.when(i % 2 == 0)
    def _():
        step(even, odd)

    @pl.when(i % 2 == 1)
    def _():
        step(odd, even)


def _moba(proj, batch, seq, heads):
    n_t, m, _ = proj.shape
    hd, blk = MOBA_HD, MOBA_BLOCK
    hp = MOBA_HEADS_PER_STEP
    assert seq % blk == 0 and heads * hd == IN_TILE and heads % hp == 0
    n_blocks = seq // blk
    topk = min(MOBA_TOPK, n_blocks - 1)
    wide = hp * hd
    kern = functools.partial(_moba_kernel, n_blocks=n_blocks, topk=topk, heads=hp)
    tile_state = [pltpu.VMEM((hp, n_blocks, blk), F32),
                  pltpu.VMEM((hp, n_blocks, blk, blk), F32),
                  pltpu.VMEM((hp, 1, blk), F32),
                  pltpu.VMEM((hp, 1, blk), F32),
                  pltpu.VMEM((hp, hd, blk), F32)]
    q_tile = lambda i: jnp.minimum(i, n_blocks - 1)
    o_tile = lambda i: jnp.maximum(i - 1, 0)
    return pl.pallas_call(
        kern,
        grid=(batch, heads // hp, n_blocks + 1),
        in_specs=[pl.BlockSpec((None, blk, wide), lambda b, g, i: (3, b * n_blocks + q_tile(i), g)),
                  pl.BlockSpec((None, seq, wide), lambda b, g, i: (4, b, g)),
                  pl.BlockSpec((None, seq, wide), lambda b, g, i: (5, b, g)),
                  pl.BlockSpec((None, seq, wide), lambda b, g, i: (3, b, g))],
        out_specs=pl.BlockSpec((blk, wide), lambda b, g, i: (b * n_blocks + o_tile(i), g)),
        out_shape=jax.ShapeDtypeStruct((m, IN_TILE), BF16),
        scratch_shapes=[pltpu.VMEM((hp, n_blocks, hd), F32),
                        pltpu.VMEM((hp, hd, seq), BF16),
                        pltpu.VMEM((blk, blk), F32),
                        pltpu.VMEM((hp, n_blocks, n_blocks, blk), F32)] + tile_state + tile_state,
        compiler_params=_params("arbitrary", "arbitrary", "arbitrary"),
        name="moba",
    )(proj, proj, proj, proj)


def _merge_kernel(og_ref, om_ref, wg_ref, wm_ref, ga_ref, gb_ref, o_ref, wgb_ref, wmb_ref):
    @pl.when(pl.program_id(1) == 0)
    def _():
        wgb_ref[...] = wg_ref[...].astype(BF16)
        wmb_ref[...] = wm_ref[...].astype(BF16)

    ya = _dot(og_ref[...], wgb_ref[...])
    yb = _dot(om_ref[...], wmb_ref[...])
    ga = jax.nn.sigmoid(ga_ref[...].astype(F32))
    gb = jax.nn.sigmoid(gb_ref[...].astype(F32))
    o_ref[...] = (ga * ya + gb * yb).astype(o_ref.dtype)


def _merge(o_gla, o_moba, w_gla_out, w_moba_out, proj, l, d, gate_a0):
    m, kg = o_gla.shape
    km = o_moba.shape[1]
    tm, tn = 1024, IN_TILE
    n_tiles = d // tn
    gate_b0 = gate_a0 + n_tiles
    return pl.pallas_call(
        _merge_kernel,
        grid=(n_tiles, m // tm),
        in_specs=[pl.BlockSpec((tm, kg), lambda j, i: (i, 0)),
                  pl.BlockSpec((tm, km), lambda j, i: (i, 0)),
                  pl.BlockSpec((None, kg, tn), lambda j, i: (l, 0, j)),
                  pl.BlockSpec((None, km, tn), lambda j, i: (l, 0, j)),
                  pl.BlockSpec((None, tm, tn), lambda j, i: (gate_a0 + j, i, 0)),
                  pl.BlockSpec((None, tm, tn), lambda j, i: (gate_b0 + j, i, 0))],
        out_specs=pl.BlockSpec((tm, tn), lambda j, i: (i, j)),
        out_shape=jax.ShapeDtypeStruct((m, d), BF16),
        scratch_shapes=[pltpu.VMEM((kg, tn), BF16), pltpu.VMEM((km, tn), BF16)],
        compiler_params=_params("arbitrary", "arbitrary"),
        name="branch_merge",
    )(o_gla, o_moba, w_gla_out, w_moba_out, proj, proj)


def _matmul_kernel(a_ref, w_ref, o_ref, wbf_ref):
    @pl.when(pl.program_id(1) == 0)
    def _():
        wbf_ref[...] = w_ref[...].astype(BF16)

    o_ref[...] = _dot(a_ref[...], wbf_ref[...]).astype(o_ref.dtype)


def _matmul(a, w, l, tm, tn, name):
    m, k = a.shape
    n = w.shape[-1]
    return pl.pallas_call(
        _matmul_kernel,
        grid=(n // tn, m // tm),
        in_specs=[pl.BlockSpec((tm, k), lambda j, i: (i, 0)),
                  pl.BlockSpec((None, k, tn), lambda j, i: (l, 0, j))],
        out_specs=pl.BlockSpec((tm, tn), lambda j, i: (i, j)),
        out_shape=jax.ShapeDtypeStruct((m, n), BF16),
        scratch_shapes=[pltpu.VMEM((k, tn), BF16)],
        compiler_params=_params("arbitrary", "arbitrary"),
        name=name,
    )(a, w)


def _ffn_row_permutation(tile):
    g = tile // V7X_SUBLANES
    pos = np.arange(tile)
    time = (pos % V7X_SUBLANES) * g + pos // V7X_SUBLANES
    perm = np.zeros((tile, tile), np.float32)
    perm[pos, time] = 1.0
    return perm


def _post_mix_kernel(x_ref, y_ref, gpost_ref, mod_ref, gpre_ref, perm_ref, o_ref, h_ref,
                     *, gate_row, shift_row, scale_row):
    y = y_ref[...].astype(F32)
    x = x_ref[...] + mod_ref[gate_row:gate_row + 1, :] * _rms(y, gpost_ref[...])
    o_ref[...] = x
    h = _rms(x, gpre_ref[...])
    h = h * (1.0 + mod_ref[scale_row:scale_row + 1, :]) + mod_ref[shift_row:shift_row + 1, :]
    h_ref[...] = _dot(perm_ref[...], h.astype(BF16)).astype(h_ref.dtype)


def _post_mix(x2, y, gpost3, mod3, gpre3, l, seq):
    m, d = x2.shape
    tm = FFN_TILE
    per_batch = seq // tm
    row_spec = pl.BlockSpec((tm, d), lambda i: (i, 0))
    gain_spec = pl.BlockSpec((None, 1, d), lambda i: (l, 0, 0))
    perm = jnp.asarray(_ffn_row_permutation(tm), BF16)
    return pl.pallas_call(
        functools.partial(_post_mix_kernel, gate_row=2, shift_row=3, scale_row=4),
        grid=(m // tm,),
        in_specs=[row_spec, row_spec, gain_spec,
                  pl.BlockSpec((None, N_MOD, d), lambda i: (i // per_batch, 0, 0)),
                  gain_spec,
                  pl.BlockSpec((tm, tm), lambda i: (0, 0))],
        out_specs=(row_spec, row_spec),
        out_shape=(jax.ShapeDtypeStruct((m, d), F32), jax.ShapeDtypeStruct((m, d), BF16)),
        compiler_params=_params("arbitrary"),
        name="post_mix_pre_ffn",
    )(x2, y, gpost3, mod3, gpre3, perm)


def _post_ffn_kernel(x_ref, y_ref, gpost_ref, mod_ref, unperm_ref, o_ref, *, gate_row):
    y = _dot(unperm_ref[...], y_ref[...])
    o_ref[...] = x_ref[...] + mod_ref[gate_row:gate_row + 1, :] * _rms(y, gpost_ref[...])


def _post_ffn(x2, y, gpost3, mod3, l, seq):
    m, d = x2.shape
    tm = FFN_TILE
    per_batch = seq // tm
    row_spec = pl.BlockSpec((tm, d), lambda i: (i, 0))
    unperm = jnp.asarray(_ffn_row_permutation(tm).T, BF16)
    return pl.pallas_call(
        functools.partial(_post_ffn_kernel, gate_row=5),
        grid=(m // tm,),
        in_specs=[row_spec, row_spec,
                  pl.BlockSpec((None, 1, d), lambda i: (l, 0, 0)),
                  pl.BlockSpec((None, N_MOD, d), lambda i: (i // per_batch, 0, 0)),
                  pl.BlockSpec((tm, tm), lambda i: (0, 0))],
        out_specs=row_spec,
        out_shape=jax.ShapeDtypeStruct((m, d), F32),
        compiler_params=_params("arbitrary"),
        name="post_ffn",
    )(x2, y, gpost3, mod3, unperm)


def _ffn_up_kernel(h_ref, wu_ref, wg_ref, cu_ref, cg_ref, bu_ref, bg_ref, o_ref,
                   wbf_ref, up_even_ref, up_odd_ref, *, row_tiles, tiles_per_batch, n_pairs):
    s = pl.program_id(0)
    tm, tn = o_ref.shape
    head = 2 * V7X_SUBLANES
    g_lane0 = tn + FFN_PAD_LANES
    matmul_starts_batch = (s % row_tiles) % tiles_per_batch == 0

    @pl.when(s == 0)
    def _():
        up_even_ref[...] = jnp.zeros_like(up_even_ref)
        up_odd_ref[...] = jnp.zeros_like(up_odd_ref)

    @pl.when(jnp.logical_and(s % row_tiles == 0, s < n_pairs))
    def _():
        wbf_ref[:, :tn] = wu_ref[...].astype(BF16)
        wbf_ref[:, tn:] = wg_ref[...].astype(BF16)

    def step(cur_ref, prev_ref):
        sub = V7X_SUBLANES

        def conv(r0, half, w_ref, b_ref):
            cols = slice(half * g_lane0, half * g_lane0 + tn)
            acc = b_ref[...]
            for tap in range(CONV_WIDTH):
                start = pl.multiple_of(r0 + tap * sub, sub)
                acc = acc + w_ref[tap:tap + 1, :] * prev_ref[pl.ds(start, FFN_ROW_GROUP), cols]
            return acc

        def chunk(c, carry):
            r0 = pl.multiple_of(c * FFN_ROW_CHUNK, FFN_ROW_CHUNK)
            up = _dot(h_ref[pl.ds(r0, FFN_ROW_CHUNK), :], wbf_ref[...])
            cur_ref[pl.ds(r0 + head, FFN_ROW_CHUNK), 0:tn] = up[:, :tn]
            cur_ref[pl.ds(r0 + head, FFN_ROW_CHUNK), g_lane0:g_lane0 + tn] = up[:, tn:]
            for r in range(0, FFN_ROW_CHUNK, FFN_ROW_GROUP):
                rg = pl.multiple_of(r0 + r, FFN_ROW_GROUP)
                u = conv(rg, 0, cu_ref, bu_ref)
                g = conv(rg, 1, cg_ref, bg_ref)
                gelu = 0.5 * g * (1.0 + jnp.tanh(np.sqrt(2.0 / np.pi) * (g + 0.044715 * (g * g * g))))
                o_ref[pl.ds(rg, FFN_ROW_GROUP), :] = (gelu * u).astype(o_ref.dtype)
            return carry

        lax.fori_loop(0, tm // FFN_ROW_CHUNK, chunk, 0)

        own = pltpu.roll(cur_ref[tm:tm + head, :], 1, axis=0)
        before = jnp.where(matmul_starts_batch, 0.0,
                           pltpu.roll(prev_ref[tm:tm + head, :], 1 + V7X_SUBLANES, axis=0))
        first = lax.broadcasted_iota(jnp.int32, (head, 1), 0) % V7X_SUBLANES == 0
        cur_ref[0:head, :] = jnp.where(first, before, own)

    @pl.when(s % 2 == 0)
    def _():
        step(up_even_ref, up_odd_ref)

    @pl.when(s % 2 == 1)
    def _():
        step(up_odd_ref, up_even_ref)


def _ffn_up(h, w_up, w_conv, b_conv3, l, seq):
    m, d = h.shape
    f = w_up.shape[-1] // 2
    tm, tn = FFN_TILE, 512
    nt = f // tn
    row_tiles = m // tm
    n_pairs = nt * row_tiles
    kern = functools.partial(_ffn_up_kernel, row_tiles=row_tiles, tiles_per_batch=seq // tm,
                             n_pairs=n_pairs)
    mm = lambda s: jnp.minimum(s, n_pairs - 1)
    ep = lambda s: jnp.maximum(s - 1, 0)
    return pl.pallas_call(
        kern,
        grid=(n_pairs + 1,),
        in_specs=[pl.BlockSpec((tm, d), lambda s: (mm(s) % row_tiles, 0)),
                  pl.BlockSpec((None, d, tn), lambda s: (l, 0, mm(s) // row_tiles)),
                  pl.BlockSpec((None, d, tn), lambda s: (l, 0, nt + mm(s) // row_tiles)),
                  pl.BlockSpec((None, CONV_WIDTH, tn), lambda s: (l, 0, ep(s) // row_tiles)),
                  pl.BlockSpec((None, CONV_WIDTH, tn), lambda s: (l, 0, nt + ep(s) // row_tiles)),
                  pl.BlockSpec((None, 1, tn), lambda s: (l, 0, ep(s) // row_tiles)),
                  pl.BlockSpec((None, 1, tn), lambda s: (l, 0, nt + ep(s) // row_tiles))],
        out_specs=pl.BlockSpec((tm, tn), lambda s: (ep(s) % row_tiles, ep(s) // row_tiles)),
        out_shape=jax.ShapeDtypeStruct((m, f), BF16),
        scratch_shapes=[pltpu.VMEM((d, 2 * tn), BF16),
                        pltpu.VMEM((tm + 2 * V7X_SUBLANES, 2 * tn + FFN_PAD_LANES), F32),
                        pltpu.VMEM((tm + 2 * V7X_SUBLANES, 2 * tn + FFN_PAD_LANES), F32)],
        compiler_params=_params("arbitrary"),
        name="ffn_up_conv_gate",
    )(h, w_up, w_up, w_conv, w_conv, b_conv3, b_conv3)


def kernel(x, c, w_ada, b_ada, g_pre_mix, w_in, w_alpha_up, b_alpha, g_gla_head, w_gla_out,
           w_moba_out, w_o, g_post_mix, g_pre_ffn, w_up, w_conv, b_conv, w_down, g_post_ffn):
    batch, seq, d = x.shape
    depth = w_ada.shape[0]
    m = batch * seq
    gla_k = w_alpha_up.shape[-1]
    gla_heads = gla_k // GLA_DK
    gla_dv = g_gla_head.shape[-1]
    gla_v = w_gla_out.shape[1]
    moba_w = w_moba_out.shape[1]
    moba_heads = moba_w // MOBA_HD
    low_col = 2 * gla_k + 2 * gla_v
    assert gla_v == gla_heads * gla_dv and d % IN_TILE == 0
    assert low_col % IN_TILE == 0 and w_in.shape[-1] == low_col + GLA_RANK + 3 * moba_w + 2 * d
    first_shifted = low_col // IN_TILE
    n_tiles = first_shifted + 3 * moba_w // IN_TILE + 2 * d // IN_TILE

    x2 = x.reshape(m, d)
    c8 = jnp.pad(c, ((0, 8 - batch), (0, 0)))
    as3 = lambda a: a.reshape(a.shape[0], 1, a.shape[-1])
    wup_pad = jnp.pad(w_alpha_up, ((0, 0), (0, V7X_LANES - GLA_RANK), (0, 0)))
    w_in_t = jnp.swapaxes(w_in, 1, 2)

    for l in range(depth):
        mod = _modulation(c8, w_ada, as3(b_ada), l)
        mod3 = mod[:batch].reshape(batch, N_MOD, d)
        h, log_a = _prenorm(x2, as3(g_pre_mix), mod3, w_in_t, wup_pad, as3(b_alpha), l, seq,
                            shift_row=0, scale_row=1, low_block=low_col // V7X_LANES)
        proj = _in_projection(h, w_in_t, l, n_tiles, first_shifted)
        o_gla = _gla(proj, log_a, as3(g_gla_head), l, batch, seq, gla_heads, gla_dv)
        o_moba = _moba(proj, batch, seq, moba_heads)
        merged = _merge(o_gla, o_moba, w_gla_out, w_moba_out, proj, l, d,
                        gate_a0=first_shifted + 3 * moba_w // IN_TILE)
        y = _matmul(merged, w_o, l, 1024, 1024, "out_projection")
        x2, h2 = _post_mix(x2, y, as3(g_post_mix), mod3, as3(g_pre_ffn), l, seq)
        act = _ffn_up(h2, w_up, w_conv, as3(b_conv), l, seq)
        y2 = _matmul(act, w_down, l, 512, 512, "ffn_down")
        x2 = _post_ffn(x2, y2, as3(g_post_ffn), mod3, l, seq)
    return x2.reshape(batch, seq, d)
```

```python
import functools

import numpy as np
import jax
import jax.numpy as jnp
from jax import lax
from jax.experimental import pallas as pl
from jax.experimental.pallas import tpu as pltpu

F32 = jnp.float32
BF16 = jnp.bfloat16

GLA_DK = 128
GLA_RANK = 16
GLA_GATE_TEMP = 16.0
GLA_CHUNK = 64
GLA_CHUNKS_PER_TRIP = 4
MOBA_HD = 128
MOBA_BLOCK = 256
MOBA_TOPK = 3
MOBA_HEADS_PER_STEP = 2
CONV_WIDTH = 3
N_MOD = 6
EPS = 1e-6

V7X_LANES = 128
V7X_SUBLANES = 8
V7X_VMEM_LIMIT_BYTES = 56 * 1024 * 1024

IN_TILE = 1024
MASK_VALUE = -1e30
LOG2E = 1.4426950408889634
FFN_TILE = 512
FFN_ROW_CHUNK = 256
FFN_ROW_GROUP = 16
FFN_PAD_LANES = 128


def _params(*semantics):
    return pltpu.CompilerParams(dimension_semantics=semantics,
                                vmem_limit_bytes=V7X_VMEM_LIMIT_BYTES)


def _dot(a, b):
    return jnp.dot(a, b, preferred_element_type=F32)


def _dot_nt(a, b):
    return lax.dot_general(a, b, (((1,), (1,)), ((), ())), preferred_element_type=F32)


def _dot_tn(a, b):
    return lax.dot_general(a, b, (((0,), (0,)), ((), ())), preferred_element_type=F32)


def _split3(x):
    x1 = x.astype(BF16)
    r1 = x - x1.astype(F32)
    x2 = r1.astype(BF16)
    x3 = (r1 - x2.astype(F32)).astype(BF16)
    return x1, x2, x3


def _rms(x, gain):
    return x * lax.rsqrt(jnp.mean(x * x, axis=-1, keepdims=True) + EPS) * gain


def _mod_kernel(c_ref, w_ref, b_ref, o_ref):
    c = c_ref[...]
    s = c * jax.nn.sigmoid(c)
    o_ref[...] = _dot(s.astype(BF16), w_ref[...].astype(BF16)) + b_ref[...]


def _modulation(c8, w_ada, b_ada3, l):
    rows, d = c8.shape
    n = w_ada.shape[-1]
    tn = 1024
    return pl.pallas_call(
        _mod_kernel,
        grid=(n // tn,),
        in_specs=[pl.BlockSpec((rows, d), lambda j: (0, 0)),
                  pl.BlockSpec((None, d, tn), lambda j: (l, 0, j)),
                  pl.BlockSpec((None, 1, tn), lambda j: (l, 0, j))],
        out_specs=pl.BlockSpec((rows, tn), lambda j: (0, j)),
        out_shape=jax.ShapeDtypeStruct((rows, n), F32),
        compiler_params=_params("arbitrary"),
        name="adaln_mod",
    )(c8, w_ada, b_ada3)


def _prenorm_kernel(x_ref, g_ref, mod_ref, w_ref, wup_ref, b_ref, o_ref, la_ref,
                    *, shift_row, scale_row):
    y = _rms(x_ref[...], g_ref[...])
    sh = mod_ref[shift_row:shift_row + 1, :]
    sc = mod_ref[scale_row:scale_row + 1, :]
    h = (y * (1.0 + sc) + sh).astype(o_ref.dtype)
    o_ref[...] = h
    low = _dot_nt(h, w_ref[...].astype(BF16))
    z = _dot(low.astype(BF16), wup_ref[...].astype(BF16)) + b_ref[...]
    log_sig = jnp.minimum(z, 0.0) - jnp.log1p(jnp.exp(-jnp.abs(z)))
    la_ref[...] = log_sig / GLA_GATE_TEMP


def _prenorm(x2, gain3, mod3, w_in_t, wup_pad, b_alpha3, l, seq, shift_row, scale_row, low_block):
    m, d = x2.shape
    k = wup_pad.shape[-1]
    tm = 512
    per_batch = seq // tm
    return pl.pallas_call(
        functools.partial(_prenorm_kernel, shift_row=shift_row, scale_row=scale_row),
        grid=(m // tm,),
        in_specs=[pl.BlockSpec((tm, d), lambda i: (i, 0)),
                  pl.BlockSpec((None, 1, d), lambda i: (l, 0, 0)),
                  pl.BlockSpec((None, N_MOD, d), lambda i: (i // per_batch, 0, 0)),
                  pl.BlockSpec((None, V7X_LANES, d), lambda i: (l, low_block, 0)),
                  pl.BlockSpec((None, V7X_LANES, k), lambda i: (l, 0, 0)),
                  pl.BlockSpec((None, 1, k), lambda i: (l, 0, 0))],
        out_specs=(pl.BlockSpec((tm, d), lambda i: (i, 0)),
                   pl.BlockSpec((tm, k), lambda i: (i, 0))),
        out_shape=(jax.ShapeDtypeStruct((m, d), BF16), jax.ShapeDtypeStruct((m, k), F32)),
        compiler_params=_params("arbitrary"),
        name="prenorm_mix",
    )(x2, gain3, mod3, w_in_t, wup_pad, b_alpha3)


def _inproj_kernel(h_ref, wa_ref, wb_ref, o_ref, wbf_ref, *, first_shifted):
    j = pl.program_id(0)
    i = pl.program_id(1)
    keep = IN_TILE - GLA_RANK

    @pl.when(jnp.logical_and(i == 0, j < first_shifted))
    def _():
        wbf_ref[...] = wa_ref[...].astype(BF16)

    @pl.when(jnp.logical_and(i == 0, j >= first_shifted))
    def _():
        wbf_ref[:keep, :] = wa_ref[GLA_RANK:, :].astype(BF16)
        wbf_ref[keep:, :] = wb_ref[...].astype(BF16)

    o_ref[...] = _dot_nt(h_ref[...], wbf_ref[...]).astype(o_ref.dtype)


def _in_projection(h, w_in_t, l, n_tiles, first_shifted):
    m, d = h.shape
    tm = 1024
    rank_blocks = IN_TILE // GLA_RANK
    return pl.pallas_call(
        functools.partial(_inproj_kernel, first_shifted=first_shifted),
        grid=(n_tiles, m // tm),
        in_specs=[pl.BlockSpec((tm, d), lambda j, i: (i, 0)),
                  pl.BlockSpec((None, IN_TILE, d), lambda j, i: (l, j, 0)),
                  pl.BlockSpec((None, GLA_RANK, d), lambda j, i: (l, (j + 1) * rank_blocks, 0))],
        out_specs=pl.BlockSpec((None, tm, IN_TILE), lambda j, i: (j, i, 0)),
        out_shape=jax.ShapeDtypeStruct((n_tiles, m, IN_TILE), BF16),
        scratch_shapes=[pltpu.VMEM((IN_TILE, d), BF16)],
        compiler_params=_params("arbitrary", "arbitrary"),
        name="in_projection",
    )(h, w_in_t, w_in_t)


def _gla_tables(c):
    levels = int(np.log2(c))
    t = np.arange(c)
    tabs = [(t[None, :] <= t[:, None]),
            (t[None, :] > t[:, None])]
    for lv in range(1, levels + 1):
        size, half = 1 << lv, 1 << (lv - 1)
        mid = (t // size) * size + half
        second = t >= mid
        q_side = second[:, None] & (t[None, :] >= mid[:, None]) & (t[None, :] <= t[:, None])
        k_side = (~second)[:, None] & (t[None, :] > t[:, None]) & (t[None, :] < mid[:, None])
        tabs.append(q_side | k_side)
    e = np.concatenate(tabs, axis=0).astype(np.float32)
    x = t[:, None] ^ t[None, :]
    lvl = np.where(x == 0, 0, np.floor(np.log2(np.maximum(x, 1))).astype(np.int64) + 1)
    lvl = np.where(t[:, None] >= t[None, :], lvl, -1).astype(np.int32)
    return e, lvl, levels


def _gla_kernel(q_ref, k_ref, v_ref, r_ref, la_ref, gh_ref, e_ref, lvl_ref, o_ref, st_ref,
                *ex_refs, heads, dv, levels, n_chunks):
    c = GLA_CHUNK
    dk = GLA_DK

    @pl.when(pl.program_id(1) == 0)
    def _():
        st_ref[...] = jnp.zeros_like(st_ref)

    e_tab = e_ref[...]
    lvl = lvl_ref[...]
    row = lax.broadcasted_iota(jnp.int32, (c, 1), 0)
    gain = gh_ref[...]

    def chunk(ci, ex_ref):
        rows = pl.ds(pl.multiple_of(ci * c, c), c)
        g3 = jnp.concatenate(_split3(la_ref[rows, :]), axis=0)
        ex_ref[...] = jnp.exp(_dot(e_tab, g3))
        for h in range(heads):
            sl = slice(h * dk, (h + 1) * dk)
            vs = slice(h * dv, (h + 1) * dv)
            qh = q_ref[rows, sl].astype(F32) * (dk ** -0.5)
            kh = k_ref[rows, sl].astype(F32)
            vh = v_ref[rows, vs]
            st = st_ref[h]
            o = _dot_nt((qh * ex_ref[0:c, sl]).astype(BF16), st.astype(BF16))
            a = jnp.where(lvl == 0, _dot_nt(qh.astype(BF16), kh.astype(BF16)), 0.0)
            for lv in range(1, levels + 1):
                e_lv = ex_ref[(1 + lv) * c:(2 + lv) * c, sl]
                second = ((row >> (lv - 1)) & 1) == 1
                ql = jnp.where(second, qh * e_lv, 0.0).astype(BF16)
                kl = jnp.where(second, 0.0, kh * e_lv).astype(BF16)
                a = a + jnp.where(lvl == lv, _dot_nt(ql, kl), 0.0)
            o = o + _dot(a.astype(BF16), vh)
            kt = (kh * ex_ref[c:2 * c, sl]).astype(BF16)
            st_ref[h] = st * ex_ref[c - 1:c, sl] + _dot_tn(vh, kt)
            y = _rms(o, gain)
            r = r_ref[rows, vs].astype(F32)
            o_ref[rows, vs] = (y * (r * jax.nn.sigmoid(r))).astype(o_ref.dtype)

    def chunk_group(gi, carry):
        for u in range(GLA_CHUNKS_PER_TRIP):
            chunk(gi * GLA_CHUNKS_PER_TRIP + u, ex_refs[u])
        return carry

    lax.fori_loop(0, n_chunks // GLA_CHUNKS_PER_TRIP, chunk_group, 0)


def _gla(proj, log_a, g_head3, l, batch, seq, heads, dv):
    n_t, m, _ = proj.shape
    dk = GLA_DK
    rows = 512
    per_batch = seq // rows
    e_np, lvl_np, levels = _gla_tables(GLA_CHUNK)
    e_np = np.concatenate([e_np, e_np, e_np], axis=1)
    e_tab = jnp.asarray(e_np, BF16)
    lvl = jnp.asarray(lvl_np)
    kw = heads * dk
    assert 2 * kw == IN_TILE and heads * dv == IN_TILE
    kern = functools.partial(_gla_kernel, heads=heads, dv=dv, levels=levels,
                             n_chunks=rows // GLA_CHUNK)
    return pl.pallas_call(
        kern,
        grid=(batch, per_batch),
        in_specs=[pl.BlockSpec((None, rows, kw), lambda b, r: (0, b * per_batch + r, 0)),
                  pl.BlockSpec((None, rows, kw), lambda b, r: (0, b * per_batch + r, 1)),
                  pl.BlockSpec((None, rows, IN_TILE), lambda b, r: (1, b * per_batch + r, 0)),
                  pl.BlockSpec((None, rows, IN_TILE), lambda b, r: (2, b * per_batch + r, 0)),
                  pl.BlockSpec((rows, kw), lambda b, r: (b * per_batch + r, 0)),
                  pl.BlockSpec((None, 1, dv), lambda b, r: (l, 0, 0)),
                  pl.BlockSpec(e_np.shape, lambda b, r: (0, 0)),
                  pl.BlockSpec(lvl_np.shape, lambda b, r: (0, 0))],
        out_specs=pl.BlockSpec((rows, IN_TILE), lambda b, r: (b * per_batch + r, 0)),
        out_shape=jax.ShapeDtypeStruct((m, IN_TILE), BF16),
        scratch_shapes=[pltpu.VMEM((heads, dv, dk), F32)]
        + [pltpu.VMEM((e_np.shape[0], kw), F32)] * GLA_CHUNKS_PER_TRIP,
        compiler_params=_params("arbitrary", "arbitrary"),
        name="gla",
    )(proj, proj, proj, proj, log_a, g_head3, e_tab, lvl)


def _moba_kernel(q_ref, k_ref, v_ref, qall_ref, o_ref, kmean_ref, vt_ref, causal_ref, bias_ref,
                 *tile_state,
                 n_blocks, topk, heads):
    blk, hd = MOBA_BLOCK, MOBA_HD
    i = pl.program_id(2)
    c2 = (hd ** -0.5) * LOG2E
    head_lanes = [slice(h * hd, (h + 1) * hd) for h in range(heads)]
    per_tile = len(tile_state) // 2
    even, odd = tile_state[:per_tile], tile_state[per_tile:]
    has_tile = i < n_blocks

    @pl.when(i == 0)
    def _():
        for h, lanes in enumerate(head_lanes):
            for n in range(n_blocks):
                rows = slice(n * blk, (n + 1) * blk)
                kmean_ref[h, n:n + 1, :] = jnp.mean(k_ref[rows, lanes].astype(F32), axis=0, keepdims=True)
                vt_ref[h, :, rows] = v_ref[rows, lanes].astype(F32).T.astype(BF16)

        kpos = lax.broadcasted_iota(jnp.int32, (blk, blk), 0)
        qpos = lax.broadcasted_iota(jnp.int32, (blk, blk), 1)
        causal_ref[...] = jnp.where(kpos <= qpos, 0.0, MASK_VALUE)
        for mblk_ref, _, _, l_ref, acc_ref in (even, odd):
            mblk_ref[...] = jnp.zeros_like(mblk_ref)
            l_ref[...] = jnp.ones_like(l_ref)
            acc_ref[...] = jnp.zeros_like(acc_ref)

        for h, lanes in enumerate(head_lanes):
            m1, m2, m3 = _split3(kmean_ref[h])
            for t in range(n_blocks):
                q = qall_ref[t * blk:(t + 1) * blk, lanes]
                s_blk = _dot_nt(m1, q) + _dot_nt(m2, q) + _dot_nt(m3, q)
                bidx = lax.broadcasted_iota(jnp.int32, s_blk.shape, 0)
                cur = jnp.where(bidx < t, s_blk, MASK_VALUE)
                sel = jnp.zeros(s_blk.shape, jnp.bool_)
                for _ in range(topk):
                    mx = jnp.max(cur, axis=0, keepdims=True)
                    idx = jnp.min(jnp.where(cur == mx, bidx, n_blocks), axis=0, keepdims=True)
                    pick = bidx == idx
                    sel = jnp.logical_or(sel, jnp.logical_and(pick, idx < t))
                    cur = jnp.where(pick, -jnp.inf, cur)
                bias_ref[h, t] = jnp.where(jnp.logical_or(sel, bidx == t), 0.0, MASK_VALUE)

    def pairwise(n, body):
        def four(t, carry):
            for u in range(4):
                body(4 * t + u)
            return carry

        lax.fori_loop(0, lax.shift_right_logical(n, 2), four, 0)
        done = n - n % 4

        @pl.when(n % 4 >= 2)
        def _():
            body(done)
            body(done + 1)

        @pl.when(n % 2 == 1)
        def _():
            body(n - 1)

    def step(this, last):
        mblk_ref, s_ref, m_ref, l_ref, acc_ref = this
        _, s_last, m_last, l_last, acc_last = last
        i_last = jnp.maximum(i - 1, 0)

        def score(j, causal):
            cols = pl.ds(pl.multiple_of(j * blk, blk), blk)
            for h, lanes in enumerate(head_lanes):
                s = _dot_nt(k_ref[cols, lanes], q_ref[:, lanes]) * c2
                if causal:
                    s = s + causal_ref[...]
                s_ref[h, j] = s
                mblk_ref[h, pl.ds(j, 1), :] = jnp.max(s, axis=0, keepdims=True)

        def attend(j):
            cols = pl.ds(pl.multiple_of(j * blk, blk), blk)
            for h in range(heads):
                p = jnp.exp2(s_last[h, j] + (bias_ref[h, i_last, pl.ds(j, 1), :] - m_last[h]))
                l_last[h] += jnp.sum(p, axis=0, keepdims=True)
                acc_last[h] += _dot(vt_ref[h, :, cols], p.astype(BF16))

        def both(j):
            score(j, False)
            attend(j)

        @pl.when(has_tile)
        def _():
            pairwise(i, both)
            score(i, True)
            for h in range(heads):
                bidx = lax.broadcasted_iota(jnp.int32, (n_blocks, blk), 0)
                allowed_max = jnp.where(bidx <= i, mblk_ref[h] + bias_ref[h, i], -jnp.inf)
                m_ref[h] = jnp.max(allowed_max, axis=0, keepdims=True)
                l_ref[h] = jnp.zeros((1, blk), F32)
                acc_ref[h] = jnp.zeros((hd, blk), F32)

        @pl.when(jnp.logical_not(has_tile))
        def _():
            pairwise(i, attend)

        for h, lanes in enumerate(head_lanes):
            o_ref[:, lanes] = (acc_last[h] / l_last[h]).T.astype(o_ref.dtype)

    @pl.when(i % 2 == 0)
    def _():
        step(even, odd)

    @pl.when(i % 2 == 1)
    def _():
        step(odd, even)


def _moba(proj, batch, seq, heads):
    n_t, m, _ = proj.shape
    hd, blk = MOBA_HD, MOBA_BLOCK
    hp = MOBA_HEADS_PER_STEP
    assert seq % blk == 0 and heads * hd == IN_TILE and heads % hp == 0
    n_blocks = seq // blk
    topk = min(MOBA_TOPK, n_blocks - 1)
    wide = hp * hd
    kern = functools.partial(_moba_kernel, n_blocks=n_blocks, topk=topk, heads=hp)
    tile_state = [pltpu.VMEM((hp, n_blocks, blk), F32),
                  pltpu.VMEM((hp, n_blocks, blk, blk), F32),
                  pltpu.VMEM((hp, 1, blk), F32),
                  pltpu.VMEM((hp, 1, blk), F32),
                  pltpu.VMEM((hp, hd, blk), F32)]
    q_tile = lambda i: jnp.minimum(i, n_blocks - 1)
    o_tile = lambda i: jnp.maximum(i - 1, 0)
    return pl.pallas_call(
        kern,
        grid=(batch, heads // hp, n_blocks + 1),
        in_specs=[pl.BlockSpec((None, blk, wide), lambda b, g, i: (3, b * n_blocks + q_tile(i), g)),
                  pl.BlockSpec((None, seq, wide), lambda b, g, i: (4, b, g)),
                  pl.BlockSpec((None, seq, wide), lambda b, g, i: (5, b, g)),
                  pl.BlockSpec((None, seq, wide), lambda b, g, i: (3, b, g))],
        out_specs=pl.BlockSpec((blk, wide), lambda b, g, i: (b * n_blocks + o_tile(i), g)),
        out_shape=jax.ShapeDtypeStruct((m, IN_TILE), BF16),
        scratch_shapes=[pltpu.VMEM((hp, n_blocks, hd), F32),
                        pltpu.VMEM((hp, hd, seq), BF16),
                        pltpu.VMEM((blk, blk), F32),
                        pltpu.VMEM((hp, n_blocks, n_blocks, blk), F32)] + tile_state + tile_state,
        compiler_params=_params("arbitrary", "arbitrary", "arbitrary"),
        name="moba",
    )(proj, proj, proj, proj)


def _merge_kernel(og_ref, om_ref, wg_ref, wm_ref, ga_ref, gb_ref, o_ref, wgb_ref, wmb_ref):
    @pl.when(pl.program_id(1) == 0)
    def _():
        wgb_ref[...] = wg_ref[...].astype(BF16)
        wmb_ref[...] = wm_ref[...].astype(BF16)

    ya = _dot(og_ref[...], wgb_ref[...])
    yb = _dot(om_ref[...], wmb_ref[...])
    ga = jax.nn.sigmoid(ga_ref[...].astype(F32))
    gb = jax.nn.sigmoid(gb_ref[...].astype(F32))
    o_ref[...] = (ga * ya + gb * yb).astype(o_ref.dtype)


def _merge(o_gla, o_moba, w_gla_out, w_moba_out, proj, l, d, gate_a0):
    m, kg = o_gla.shape
    km = o_moba.shape[1]
    tm, tn = 1024, IN_TILE
    n_tiles = d // tn
    gate_b0 = gate_a0 + n_tiles
    return pl.pallas_call(
        _merge_kernel,
        grid=(n_tiles, m // tm),
        in_specs=[pl.BlockSpec((tm, kg), lambda j, i: (i, 0)),
                  pl.BlockSpec((tm, km), lambda j, i: (i, 0)),
                  pl.BlockSpec((None, kg, tn), lambda j, i: (l, 0, j)),
                  pl.BlockSpec((None, km, tn), lambda j, i: (l, 0, j)),
                  pl.BlockSpec((None, tm, tn), lambda j, i: (gate_a0 + j, i, 0)),
                  pl.BlockSpec((None, tm, tn), lambda j, i: (gate_b0 + j, i, 0))],
        out_specs=pl.BlockSpec((tm, tn), lambda j, i: (i, j)),
        out_shape=jax.ShapeDtypeStruct((m, d), BF16),
        scratch_shapes=[pltpu.VMEM((kg, tn), BF16), pltpu.VMEM((km, tn), BF16)],
        compiler_params=_params("arbitrary", "arbitrary"),
        name="branch_merge",
    )(o_gla, o_moba, w_gla_out, w_moba_out, proj, proj)


def _matmul_kernel(a_ref, w_ref, o_ref, wbf_ref):
    @pl.when(pl.program_id(1) == 0)
    def _():
        wbf_ref[...] = w_ref[...].astype(BF16)

    o_ref[...] = _dot(a_ref[...], wbf_ref[...]).astype(o_ref.dtype)


def _matmul(a, w, l, tm, tn, name):
    m, k = a.shape
    n = w.shape[-1]
    return pl.pallas_call(
        _matmul_kernel,
        grid=(n // tn, m // tm),
        in_specs=[pl.BlockSpec((tm, k), lambda j, i: (i, 0)),
                  pl.BlockSpec((None, k, tn), lambda j, i: (l, 0, j))],
        out_specs=pl.BlockSpec((tm, tn), lambda j, i: (i, j)),
        out_shape=jax.ShapeDtypeStruct((m, n), BF16),
        scratch_shapes=[pltpu.VMEM((k, tn), BF16)],
        compiler_params=_params("arbitrary", "arbitrary"),
        name=name,
    )(a, w)


def _ffn_row_permutation(tile):
    g = tile // V7X_SUBLANES
    pos = np.arange(tile)
    time = (pos % V7X_SUBLANES) * g + pos // V7X_SUBLANES
    perm = np.zeros((tile, tile), np.float32)
    perm[pos, time] = 1.0
    return perm


def _post_mix_kernel(x_ref, y_ref, gpost_ref, mod_ref, gpre_ref, perm_ref, o_ref, h_ref,
                     *, gate_row, shift_row, scale_row):
    y = y_ref[...].astype(F32)
    x = x_ref[...] + mod_ref[gate_row:gate_row + 1, :] * _rms(y, gpost_ref[...])
    o_ref[...] = x
    h = _rms(x, gpre_ref[...])
    h = h * (1.0 + mod_ref[scale_row:scale_row + 1, :]) + mod_ref[shift_row:shift_row + 1, :]
    h_ref[...] = _dot(perm_ref[...], h.astype(BF16)).astype(h_ref.dtype)


def _post_mix(x2, y, gpost3, mod3, gpre3, l, seq):
    m, d = x2.shape
    tm = FFN_TILE
    per_batch = seq // tm
    row_spec = pl.BlockSpec((tm, d), lambda i: (i, 0))
    gain_spec = pl.BlockSpec((None, 1, d), lambda i: (l, 0, 0))
    perm = jnp.asarray(_ffn_row_permutation(tm), BF16)
    return pl.pallas_call(
        functools.partial(_post_mix_kernel, gate_row=2, shift_row=3, scale_row=4),
        grid=(m // tm,),
        in_specs=[row_spec, row_spec, gain_spec,
                  pl.BlockSpec((None, N_MOD, d), lambda i: (i // per_batch, 0, 0)),
                  gain_spec,
                  pl.BlockSpec((tm, tm), lambda i: (0, 0))],
        out_specs=(row_spec, row_spec),
        out_shape=(jax.ShapeDtypeStruct((m, d), F32), jax.ShapeDtypeStruct((m, d), BF16)),
        compiler_params=_params("arbitrary"),
        name="post_mix_pre_ffn",
    )(x2, y, gpost3, mod3, gpre3, perm)


def _post_ffn_kernel(x_ref, y_ref, gpost_ref, mod_ref, unperm_ref, o_ref, *, gate_row):
    y = _dot(unperm_ref[...], y_ref[...])
    o_ref[...] = x_ref[...] + mod_ref[gate_row:gate_row + 1, :] * _rms(y, gpost_ref[...])


def _post_ffn(x2, y, gpost3, mod3, l, seq):
    m, d = x2.shape
    tm = FFN_TILE
    per_batch = seq // tm
    row_spec = pl.BlockSpec((tm, d), lambda i: (i, 0))
    unperm = jnp.asarray(_ffn_row_permutation(tm).T, BF16)
    return pl.pallas_call(
        functools.partial(_post_ffn_kernel, gate_row=5),
        grid=(m // tm,),
        in_specs=[row_spec, row_spec,
                  pl.BlockSpec((None, 1, d), lambda i: (l, 0, 0)),
                  pl.BlockSpec((None, N_MOD, d), lambda i: (i // per_batch, 0, 0)),
                  pl.BlockSpec((tm, tm), lambda i: (0, 0))],
        out_specs=row_spec,
        out_shape=jax.ShapeDtypeStruct((m, d), F32),
        compiler_params=_params("arbitrary"),
        name="post_ffn",
    )(x2, y, gpost3, mod3, unperm)


def _ffn_up_kernel(h_ref, wu_ref, wg_ref, cu_ref, cg_ref, bu_ref, bg_ref, o_ref,
                   wbf_ref, up_even_ref, up_odd_ref, *, row_tiles, tiles_per_batch, n_pairs):
    s = pl.program_id(0)
    tm, tn = o_ref.shape
    head = 2 * V7X_SUBLANES
    g_lane0 = tn + FFN_PAD_LANES
    matmul_starts_batch = (s % row_tiles) % tiles_per_batch == 0

    @pl.when(s == 0)
    def _():
        up_even_ref[...] = jnp.zeros_like(up_even_ref)
        up_odd_ref[...] = jnp.zeros_like(up_odd_ref)

    @pl.when(jnp.logical_and(s % row_tiles == 0, s < n_pairs))
    def _():
        wbf_ref[:, :tn] = wu_ref[...].astype(BF16)
        wbf_ref[:, tn:] = wg_ref[...].astype(BF16)

    def step(cur_ref, prev_ref):
        sub = V7X_SUBLANES

        def conv(r0, half, w_ref, b_ref):
            cols = slice(half * g_lane0, half * g_lane0 + tn)
            acc = b_ref[...]
            for tap in range(CONV_WIDTH):
                start = pl.multiple_of(r0 + tap * sub, sub)
                acc = acc + w_ref[tap:tap + 1, :] * prev_ref[pl.ds(start, FFN_ROW_GROUP), cols]
            return acc

        def chunk(c, carry):
            r0 = pl.multiple_of(c * FFN_ROW_CHUNK, FFN_ROW_CHUNK)
            up = _dot(h_ref[pl.ds(r0, FFN_ROW_CHUNK), :], wbf_ref[...])
            cur_ref[pl.ds(r0 + head, FFN_ROW_CHUNK), 0:tn] = up[:, :tn]
            cur_ref[pl.ds(r0 + head, FFN_ROW_CHUNK), g_lane0:g_lane0 + tn] = up[:, tn:]
            for r in range(0, FFN_ROW_CHUNK, FFN_ROW_GROUP):
                rg = pl.multiple_of(r0 + r, FFN_ROW_GROUP)
                u = conv(rg, 0, cu_ref, bu_ref)
                g = conv(rg, 1, cg_ref, bg_ref)
                gelu = 0.5 * g * (1.0 + jnp.tanh(np.sqrt(2.0 / np.pi) * (g + 0.044715 * (g * g * g))))
                o_ref[pl.ds(rg, FFN_ROW_GROUP), :] = (gelu * u).astype(o_ref.dtype)
            return carry

        lax.fori_loop(0, tm // FFN_ROW_CHUNK, chunk, 0)

        own = pltpu.roll(cur_ref[tm:tm + head, :], 1, axis=0)
        before = jnp.where(matmul_starts_batch, 0.0,
                           pltpu.roll(prev_ref[tm:tm + head, :], 1 + V7X_SUBLANES, axis=0))
        first = lax.broadcasted_iota(jnp.int32, (head, 1), 0) % V7X_SUBLANES == 0
        cur_ref[0:head, :] = jnp.where(first, before, own)

    @pl.when(s % 2 == 0)
    def _():
        step(up_even_ref, up_odd_ref)

    @pl.when(s % 2 == 1)
    def _():
        step(up_odd_ref, up_even_ref)


def _ffn_up(h, w_up, w_conv, b_conv3, l, seq):
    m, d = h.shape
    f = w_up.shape[-1] // 2
    tm, tn = FFN_TILE, 512
    nt = f // tn
    row_tiles = m // tm
    n_pairs = nt * row_tiles
    kern = functools.partial(_ffn_up_kernel, row_tiles=row_tiles, tiles_per_batch=seq // tm,
                             n_pairs=n_pairs)
    mm = lambda s: jnp.minimum(s, n_pairs - 1)
    ep = lambda s: jnp.maximum(s - 1, 0)
    return pl.pallas_call(
        kern,
        grid=(n_pairs + 1,),
        in_specs=[pl.BlockSpec((tm, d), lambda s: (mm(s) % row_tiles, 0)),
                  pl.BlockSpec((None, d, tn), lambda s: (l, 0, mm(s) // row_tiles)),
                  pl.BlockSpec((None, d, tn), lambda s: (l, 0, nt + mm(s) // row_tiles)),
                  pl.BlockSpec((None, CONV_WIDTH, tn), lambda s: (l, 0, ep(s) // row_tiles)),
                  pl.BlockSpec((None, CONV_WIDTH, tn), lambda s: (l, 0, nt + ep(s) // row_tiles)),
                  pl.BlockSpec((None, 1, tn), lambda s: (l, 0, ep(s) // row_tiles)),
                  pl.BlockSpec((None, 1, tn), lambda s: (l, 0, nt + ep(s) // row_tiles))],
        out_specs=pl.BlockSpec((tm, tn), lambda s: (ep(s) % row_tiles, ep(s) // row_tiles)),
        out_shape=jax.ShapeDtypeStruct((m, f), BF16),
        scratch_shapes=[pltpu.VMEM((d, 2 * tn), BF16),
                        pltpu.VMEM((tm + 2 * V7X_SUBLANES, 2 * tn + FFN_PAD_LANES), F32),
                        pltpu.VMEM((tm + 2 * V7X_SUBLANES, 2 * tn + FFN_PAD_LANES), F32)],
        compiler_params=_params("arbitrary"),
        name="ffn_up_conv_gate",
    )(h, w_up, w_up, w_conv, w_conv, b_conv3, b_conv3)


def kernel(x, c, w_ada, b_ada, g_pre_mix, w_in, w_alpha_up, b_alpha, g_gla_head, w_gla_out,
           w_moba_out, w_o, g_post_mix, g_pre_ffn, w_up, w_conv, b_conv, w_down, g_post_ffn):
    batch, seq, d = x.shape
    depth = w_ada.shape[0]
    m = batch * seq
    gla_k = w_alpha_up.shape[-1]
    gla_heads = gla_k // GLA_DK
    gla_dv = g_gla_head.shape[-1]
    gla_v = w_gla_out.shape[1]
    moba_w = w_moba_out.shape[1]
    moba_heads = moba_w // MOBA_HD
    low_col = 2 * gla_k + 2 * gla_v
    assert gla_v == gla_heads * gla_dv and d % IN_TILE == 0
    assert low_col % IN_TILE == 0 and w_in.shape[-1] == low_col + GLA_RANK + 3 * moba_w + 2 * d
    first_shifted = low_col // IN_TILE
    n_tiles = first_shifted + 3 * moba_w // IN_TILE + 2 * d // IN_TILE

    x2 = x.reshape(m, d)
    c8 = jnp.pad(c, ((0, 8 - batch), (0, 0)))
    as3 = lambda a: a.reshape(a.shape[0], 1, a.shape[-1])
    wup_pad = jnp.pad(w_alpha_up, ((0, 0), (0, V7X_LANES - GLA_RANK), (0, 0)))
    w_in_t = jnp.swapaxes(w_in, 1, 2)

    for l in range(depth):
        mod = _modulation(c8, w_ada, as3(b_ada), l)
        mod3 = mod[:batch].reshape(batch, N_MOD, d)
        h, log_a = _prenorm(x2, as3(g_pre_mix), mod3, w_in_t, wup_pad, as3(b_alpha), l, seq,
                            shift_row=0, scale_row=1, low_block=low_col // V7X_LANES)
        proj = _in_projection(h, w_in_t, l, n_tiles, first_shifted)
        o_gla = _gla(proj, log_a, as3(g_gla_head), l, batch, seq, gla_heads, gla_dv)
        o_moba = _moba(proj, batch, seq, moba_heads)
        merged = _merge(o_gla, o_moba, w_gla_out, w_moba_out, proj, l, d,
                        gate_a0=first_shifted + 3 * moba_w // IN_TILE)
        y = _matmul(merged, w_o, l, 1024, 1024, "out_projection")
        x2, h2 = _post_mix(x2, y, as3(g_post_mix), mod3, as3(g_pre_ffn), l, seq)
        act = _ffn_up(h2, w_up, w_conv, as3(b_conv), l, seq)
        y2 = _matmul(act, w_down, l, 512, 512, "ffn_down")
        x2 = _post_ffn(x2, y2, as3(g_post_ffn), mod3, l, seq)
    return x2.reshape(batch, seq, d)
```

```python
import functools

import numpy as np
import jax
import jax.numpy as jnp
from jax import lax
from jax.experimental import pallas as pl
from jax.experimental.pallas import tpu as pltpu

F32 = jnp.float32
BF16 = jnp.bfloat16

GLA_DK = 128
GLA_RANK = 16
GLA_GATE_TEMP = 16.0
GLA_CHUNK = 64
GLA_CHUNKS_PER_TRIP = 4
MOBA_HD = 128
MOBA_BLOCK = 256
MOBA_TOPK = 3
MOBA_HEADS_PER_STEP = 2
MOBA_BLOCKS_PER_TRIP = 8
CONV_WIDTH = 3
N_MOD = 6
EPS = 1e-6

V7X_LANES = 128
V7X_SUBLANES = 8
V7X_VMEM_LIMIT_BYTES = 56 * 1024 * 1024

IN_TILE = 1024
MASK_VALUE = -1e30
LOG2E = 1.4426950408889634
FFN_TILE = 512
FFN_ROW_CHUNK = 256
FFN_ROW_GROUP = 16
FFN_PAD_LANES = 128


def _params(*semantics):
    return pltpu.CompilerParams(dimension_semantics=semantics,
                                vmem_limit_bytes=V7X_VMEM_LIMIT_BYTES)


def _dot(a, b):
    return jnp.dot(a, b, preferred_element_type=F32)


def _dot_nt(a, b):
    return lax.dot_general(a, b, (((1,), (1,)), ((), ())), preferred_element_type=F32)


def _dot_tn(a, b):
    return lax.dot_general(a, b, (((0,), (0,)), ((), ())), preferred_element_type=F32)


def _split3(x):
    x1 = x.astype(BF16)
    r1 = x - x1.astype(F32)
    x2 = r1.astype(BF16)
    x3 = (r1 - x2.astype(F32)).astype(BF16)
    return x1, x2, x3


def _rms(x, gain):
    return x * lax.rsqrt(jnp.mean(x * x, axis=-1, keepdims=True) + EPS) * gain


def _mod_kernel(c_ref, w_ref, b_ref, o_ref):
    c = c_ref[...]
    s = c * jax.nn.sigmoid(c)
    o_ref[...] = _dot(s.astype(BF16), w_ref[...].astype(BF16)) + b_ref[...]


def _modulation(c8, w_ada, b_ada3, l):
    rows, d = c8.shape
    n = w_ada.shape[-1]
    tn = 1024
    return pl.pallas_call(
        _mod_kernel,
        grid=(n // tn,),
        in_specs=[pl.BlockSpec((rows, d), lambda j: (0, 0)),
                  pl.BlockSpec((None, d, tn), lambda j: (l, 0, j)),
                  pl.BlockSpec((None, 1, tn), lambda j: (l, 0, j))],
        out_specs=pl.BlockSpec((rows, tn), lambda j: (0, j)),
        out_shape=jax.ShapeDtypeStruct((rows, n), F32),
        compiler_params=_params("arbitrary"),
        name="adaln_mod",
    )(c8, w_ada, b_ada3)


def _prenorm_kernel(x_ref, g_ref, mod_ref, w_ref, wup_ref, b_ref, o_ref, la_ref,
                    *, shift_row, scale_row):
    y = _rms(x_ref[...], g_ref[...])
    sh = mod_ref[shift_row:shift_row + 1, :]
    sc = mod_ref[scale_row:scale_row + 1, :]
    h = (y * (1.0 + sc) + sh).astype(o_ref.dtype)
    o_ref[...] = h
    low = _dot_nt(h, w_ref[...].astype(BF16))
    z = _dot(low.astype(BF16), wup_ref[...].astype(BF16)) + b_ref[...]
    log_sig = jnp.minimum(z, 0.0) - jnp.log1p(jnp.exp(-jnp.abs(z)))
    la_ref[...] = log_sig / GLA_GATE_TEMP


def _prenorm(x2, gain3, mod3, w_in_t, wup_pad, b_alpha3, l, seq, shift_row, scale_row, low_block):
    m, d = x2.shape
    k = wup_pad.shape[-1]
    tm = 512
    per_batch = seq // tm
    return pl.pallas_call(
        functools.partial(_prenorm_kernel, shift_row=shift_row, scale_row=scale_row),
        grid=(m // tm,),
        in_specs=[pl.BlockSpec((tm, d), lambda i: (i, 0)),
                  pl.BlockSpec((None, 1, d), lambda i: (l, 0, 0)),
                  pl.BlockSpec((None, N_MOD, d), lambda i: (i // per_batch, 0, 0)),
                  pl.BlockSpec((None, V7X_LANES, d), lambda i: (l, low_block, 0)),
                  pl.BlockSpec((None, V7X_LANES, k), lambda i: (l, 0, 0)),
                  pl.BlockSpec((None, 1, k), lambda i: (l, 0, 0))],
        out_specs=(pl.BlockSpec((tm, d), lambda i: (i, 0)),
                   pl.BlockSpec((tm, k), lambda i: (i, 0))),
        out_shape=(jax.ShapeDtypeStruct((m, d), BF16), jax.ShapeDtypeStruct((m, k), F32)),
        compiler_params=_params("arbitrary"),
        name="prenorm_mix",
    )(x2, gain3, mod3, w_in_t, wup_pad, b_alpha3)


def _inproj_kernel(h_ref, wa_ref, wb_ref, o_ref, wbf_ref, *, first_shifted):
    j = pl.program_id(0)
    i = pl.program_id(1)
    keep = IN_TILE - GLA_RANK

    @pl.when(jnp.logical_and(i == 0, j < first_shifted))
    def _():
        wbf_ref[...] = wa_ref[...].astype(BF16)

    @pl.when(jnp.logical_and(i == 0, j >= first_shifted))
    def _():
        wbf_ref[:keep, :] = wa_ref[GLA_RANK:, :].astype(BF16)
        wbf_ref[keep:, :] = wb_ref[...].astype(BF16)

    o_ref[...] = _dot_nt(h_ref[...], wbf_ref[...]).astype(o_ref.dtype)


def _in_projection(h, w_in_t, l, n_tiles, first_shifted):
    m, d = h.shape
    tm = 1024
    rank_blocks = IN_TILE // GLA_RANK
    return pl.pallas_call(
        functools.partial(_inproj_kernel, first_shifted=first_shifted),
        grid=(n_tiles, m // tm),
        in_specs=[pl.BlockSpec((tm, d), lambda j, i: (i, 0)),
                  pl.BlockSpec((None, IN_TILE, d), lambda j, i: (l, j, 0)),
                  pl.BlockSpec((None, GLA_RANK, d), lambda j, i: (l, (j + 1) * rank_blocks, 0))],
        out_specs=pl.BlockSpec((None, tm, IN_TILE), lambda j, i: (j, i, 0)),
        out_shape=jax.ShapeDtypeStruct((n_tiles, m, IN_TILE), BF16),
        scratch_shapes=[pltpu.VMEM((IN_TILE, d), BF16)],
        compiler_params=_params("arbitrary", "arbitrary"),
        name="in_projection",
    )(h, w_in_t, w_in_t)


def _gla_tables(c):
    levels = int(np.log2(c))
    t = np.arange(c)
    tabs = [(t[None, :] <= t[:, None]),
            (t[None, :] > t[:, None])]
    for lv in range(1, levels + 1):
        size, half = 1 << lv, 1 << (lv - 1)
        mid = (t // size) * size + half
        second = t >= mid
        q_side = second[:, None] & (t[None, :] >= mid[:, None]) & (t[None, :] <= t[:, None])
        k_side = (~second)[:, None] & (t[None, :] > t[:, None]) & (t[None, :] < mid[:, None])
        tabs.append(q_side | k_side)
    e = np.concatenate(tabs, axis=0).astype(np.float32)
    x = t[:, None] ^ t[None, :]
    lvl = np.where(x == 0, 0, np.floor(np.log2(np.maximum(x, 1))).astype(np.int64) + 1)
    lvl = np.where(t[:, None] >= t[None, :], lvl, -1).astype(np.int32)
    return e, lvl, levels


def _gla_kernel(q_ref, k_ref, v_ref, r_ref, la_ref, gh_ref, e_ref, lvl_ref, o_ref, st_ref,
                *ex_refs, heads, dv, levels, n_chunks):
    c = GLA_CHUNK
    dk = GLA_DK

    @pl.when(pl.program_id(1) == 0)
    def _():
        st_ref[...] = jnp.zeros_like(st_ref)

    e_tab = e_ref[...]
    lvl = lvl_ref[...]
    row = lax.broadcasted_iota(jnp.int32, (c, 1), 0)
    gain = gh_ref[...]

    def chunk(ci, ex_ref):
        rows = pl.ds(pl.multiple_of(ci * c, c), c)
        g3 = jnp.concatenate(_split3(la_ref[rows, :]), axis=0)
        ex_ref[...] = jnp.exp(_dot(e_tab, g3))
        for h in range(heads):
            sl = slice(h * dk, (h + 1) * dk)
            vs = slice(h * dv, (h + 1) * dv)
            qh = q_ref[rows, sl].astype(F32) * (dk ** -0.5)
            kh = k_ref[rows, sl].astype(F32)
            vh = v_ref[rows, vs]
            st = st_ref[h]
            o = _dot_nt((qh * ex_ref[0:c, sl]).astype(BF16), st.astype(BF16))
            a = jnp.where(lvl == 0, _dot_nt(qh.astype(BF16), kh.astype(BF16)), 0.0)
            for lv in range(1, levels + 1):
                e_lv = ex_ref[(1 + lv) * c:(2 + lv) * c, sl]
                second = ((row >> (lv - 1)) & 1) == 1
                ql = jnp.where(second, qh * e_lv, 0.0).astype(BF16)
                kl = jnp.where(second, 0.0, kh * e_lv).astype(BF16)
                a = a + jnp.where(lvl == lv, _dot_nt(ql, kl), 0.0)
            o = o + _dot(a.astype(BF16), vh)
            kt = (kh * ex_ref[c:2 * c, sl]).astype(BF16)
            st_ref[h] = st * ex_ref[c - 1:c, sl] + _dot_tn(vh, kt)
            y = _rms(o, gain)
            r = r_ref[rows, vs].astype(F32)
            o_ref[rows, vs] = (y * (r * jax.nn.sigmoid(r))).astype(o_ref.dtype)

    def chunk_group(gi, carry):
        for u in range(GLA_CHUNKS_PER_TRIP):
            chunk(gi * GLA_CHUNKS_PER_TRIP + u, ex_refs[u])
        return carry

    lax.fori_loop(0, n_chunks // GLA_CHUNKS_PER_TRIP, chunk_group, 0)


def _gla(proj, log_a, g_head3, l, batch, seq, heads, dv):
    n_t, m, _ = proj.shape
    dk = GLA_DK
    rows = 512
    per_batch = seq // rows
    e_np, lvl_np, levels = _gla_tables(GLA_CHUNK)
    e_np = np.concatenate([e_np, e_np, e_np], axis=1)
    e_tab = jnp.asarray(e_np, BF16)
    lvl = jnp.asarray(lvl_np)
    kw = heads * dk
    assert 2 * kw == IN_TILE and heads * dv == IN_TILE
    kern = functools.partial(_gla_kernel, heads=heads, dv=dv, levels=levels,
                             n_chunks=rows // GLA_CHUNK)
    return pl.pallas_call(
        kern,
        grid=(batch, per_batch),
        in_specs=[pl.BlockSpec((None, rows, kw), lambda b, r: (0, b * per_batch + r, 0)),
                  pl.BlockSpec((None, rows, kw), lambda b, r: (0, b * per_batch + r, 1)),
                  pl.BlockSpec((None, rows, IN_TILE), lambda b, r: (1, b * per_batch + r, 0)),
                  pl.BlockSpec((None, rows, IN_TILE), lambda b, r: (2, b * per_batch + r, 0)),
                  pl.BlockSpec((rows, kw), lambda b, r: (b * per_batch + r, 0)),
                  pl.BlockSpec((None, 1, dv), lambda b, r: (l, 0, 0)),
                  pl.BlockSpec(e_np.shape, lambda b, r: (0, 0)),
                  pl.BlockSpec(lvl_np.shape, lambda b, r: (0, 0))],
        out_specs=pl.BlockSpec((rows, IN_TILE), lambda b, r: (b * per_batch + r, 0)),
        out_shape=jax.ShapeDtypeStruct((m, IN_TILE), BF16),
        scratch_shapes=[pltpu.VMEM((heads, dv, dk), F32)]
        + [pltpu.VMEM((e_np.shape[0], kw), F32)] * GLA_CHUNKS_PER_TRIP,
        compiler_params=_params("arbitrary", "arbitrary"),
        name="gla",
    )(proj, proj, proj, proj, log_a, g_head3, e_tab, lvl)


def _moba_kernel(q_ref, k_ref, v_ref, qall_ref, o_ref, kmean_ref, vt_ref, causal_ref, bias_ref,
                 *tile_state,
                 n_blocks, topk, heads):
    blk, hd = MOBA_BLOCK, MOBA_HD
    i = pl.program_id(2)
    c2 = (hd ** -0.5) * LOG2E
    head_lanes = [slice(h * hd, (h + 1) * hd) for h in range(heads)]
    per_tile = len(tile_state) // 2
    even, odd = tile_state[:per_tile], tile_state[per_tile:]
    has_tile = i < n_blocks

    @pl.when(i == 0)
    def _():
        for h, lanes in enumerate(head_lanes):
            for n in range(n_blocks):
                rows = slice(n * blk, (n + 1) * blk)
                kmean_ref[h, n:n + 1, :] = jnp.mean(k_ref[rows, lanes].astype(F32), axis=0, keepdims=True)
                vt_ref[h, :, rows] = v_ref[rows, lanes].astype(F32).T.astype(BF16)

        kpos = lax.broadcasted_iota(jnp.int32, (blk, blk), 0)
        qpos = lax.broadcasted_iota(jnp.int32, (blk, blk), 1)
        causal_ref[...] = jnp.where(kpos <= qpos, 0.0, MASK_VALUE)
        for mblk_ref, _, _, l_ref, acc_ref in (even, odd):
            mblk_ref[...] = jnp.zeros_like(mblk_ref)
            l_ref[...] = jnp.ones_like(l_ref)
            acc_ref[...] = jnp.zeros_like(acc_ref)

        for h, lanes in enumerate(head_lanes):
            m1, m2, m3 = _split3(kmean_ref[h])
            for t in range(n_blocks):
                q = qall_ref[t * blk:(t + 1) * blk, lanes]
                s_blk = _dot_nt(m1, q) + _dot_nt(m2, q) + _dot_nt(m3, q)
                bidx = lax.broadcasted_iota(jnp.int32, s_blk.shape, 0)
                cur = jnp.where(bidx < t, s_blk, MASK_VALUE)
                sel = jnp.zeros(s_blk.shape, jnp.bool_)
                for _ in range(topk):
                    mx = jnp.max(cur, axis=0, keepdims=True)
                    idx = jnp.min(jnp.where(cur == mx, bidx, n_blocks), axis=0, keepdims=True)
                    pick = bidx == idx
                    sel = jnp.logical_or(sel, jnp.logical_and(pick, idx < t))
                    cur = jnp.where(pick, -jnp.inf, cur)
                bias_ref[h, t] = jnp.where(jnp.logical_or(sel, bidx == t), 0.0, MASK_VALUE)

    def pairwise(n, body):
        width = MOBA_BLOCKS_PER_TRIP

        def trip(t, carry):
            for u in range(width):
                body(width * t + u)
            return carry

        lax.fori_loop(0, lax.shift_right_logical(n, width.bit_length() - 1), trip, 0)
        group = width // 2
        while group >= 1:
            start = n - n % (2 * group)

            @pl.when(n % (2 * group) >= group)
            def _(start=start, group=group):
                for u in range(group):
                    body(start + u)

            group //= 2

    def step(this, last):
        mblk_ref, s_ref, m_ref, l_ref, acc_ref = this
        _, s_last, m_last, l_last, acc_last = last
        i_last = jnp.maximum(i - 1, 0)

        def score(j, causal):
            cols = pl.ds(pl.multiple_of(j * blk, blk), blk)
            for h, lanes in enumerate(head_lanes):
                s = _dot_nt(k_ref[cols, lanes], q_ref[:, lanes]) * c2
                if causal:
                    s = s + causal_ref[...]
                s_ref[h, j] = s
                mblk_ref[h, pl.ds(j, 1), :] = jnp.max(s, axis=0, keepdims=True)

        def attend(j):
            cols = pl.ds(pl.multiple_of(j * blk, blk), blk)
            for h in range(heads):
                p = jnp.exp2(s_last[h, j] + (bias_ref[h, i_last, pl.ds(j, 1), :] - m_last[h]))
                l_last[h] += jnp.sum(p, axis=0, keepdims=True)
                acc_last[h] += _dot(vt_ref[h, :, cols], p.astype(BF16))

        def both(j):
            score(j, False)
            attend(j)

        @pl.when(has_tile)
        def _():
            pairwise(i, both)
            score(i, True)
            for h in range(heads):
                bidx = lax.broadcasted_iota(jnp.int32, (n_blocks, blk), 0)
                allowed_max = jnp.where(bidx <= i, mblk_ref[h] + bias_ref[h, i], -jnp.inf)
                m_ref[h] = jnp.max(allowed_max, axis=0, keepdims=True)
                l_ref[h] = jnp.zeros((1, blk), F32)
                acc_ref[h] = jnp.zeros((hd, blk), F32)

        @pl.when(jnp.logical_not(has_tile))
        def _():
            pairwise(i, attend)

        for h, lanes in enumerate(head_lanes):
            o_ref[:, lanes] = (acc_last[h] / l_last[h]).T.astype(o_ref.dtype)

    @pl.when(i % 2 == 0)
    def _():
        step(even, odd)

    @pl.when(i % 2 == 1)
    def _():
        step(odd, even)


def _moba(proj, batch, seq, heads):
    n_t, m, _ = proj.shape
    hd, blk = MOBA_HD, MOBA_BLOCK
    hp = MOBA_HEADS_PER_STEP
    assert seq % blk == 0 and heads * hd == IN_TILE and heads % hp == 0
    n_blocks = seq // blk
    topk = min(MOBA_TOPK, n_blocks - 1)
    wide = hp * hd
    kern = functools.partial(_moba_kernel, n_blocks=n_blocks, topk=topk, heads=hp)
    tile_state = [pltpu.VMEM((hp, n_blocks, blk), F32),
                  pltpu.VMEM((hp, n_blocks, blk, blk), F32),
                  pltpu.VMEM((hp, 1, blk), F32),
                  pltpu.VMEM((hp, 1, blk), F32),
                  pltpu.VMEM((hp, hd, blk), F32)]
    q_tile = lambda i: jnp.minimum(i, n_blocks - 1)
    o_tile = lambda i: jnp.maximum(i - 1, 0)
    return pl.pallas_call(
        kern,
        grid=(batch, heads // hp, n_blocks + 1),
        in_specs=[pl.BlockSpec((None, blk, wide), lambda b, g, i: (3, b * n_blocks + q_tile(i), g)),
                  pl.BlockSpec((None, seq, wide), lambda b, g, i: (4, b, g)),
                  pl.BlockSpec((None, seq, wide), lambda b, g, i: (5, b, g)),
                  pl.BlockSpec((None, seq, wide), lambda b, g, i: (3, b, g))],
        out_specs=pl.BlockSpec((blk, wide), lambda b, g, i: (b * n_blocks + o_tile(i), g)),
        out_shape=jax.ShapeDtypeStruct((m, IN_TILE), BF16),
        scratch_shapes=[pltpu.VMEM((hp, n_blocks, hd), F32),
                        pltpu.VMEM((hp, hd, seq), BF16),
                        pltpu.VMEM((blk, blk), F32),
                        pltpu.VMEM((hp, n_blocks, n_blocks, blk), F32)] + tile_state + tile_state,
        compiler_params=_params("arbitrary", "arbitrary", "arbitrary"),
        name="moba",
    )(proj, proj, proj, proj)


def _merge_kernel(og_ref, om_ref, wg_ref, wm_ref, ga_ref, gb_ref, o_ref, wgb_ref, wmb_ref):
    @pl.when(pl.program_id(1) == 0)
    def _():
        wgb_ref[...] = wg_ref[...].astype(BF16)
        wmb_ref[...] = wm_ref[...].astype(BF16)

    ya = _dot(og_ref[...], wgb_ref[...])
    yb = _dot(om_ref[...], wmb_ref[...])
    ga = jax.nn.sigmoid(ga_ref[...].astype(F32))
    gb = jax.nn.sigmoid(gb_ref[...].astype(F32))
    o_ref[...] = (ga * ya + gb * yb).astype(o_ref.dtype)


def _merge(o_gla, o_moba, w_gla_out, w_moba_out, proj, l, d, gate_a0):
    m, kg = o_gla.shape
    km = o_moba.shape[1]
    tm, tn = 1024, IN_TILE
    n_tiles = d // tn
    gate_b0 = gate_a0 + n_tiles
    return pl.pallas_call(
        _merge_kernel,
        grid=(n_tiles, m // tm),
        in_specs=[pl.BlockSpec((tm, kg), lambda j, i: (i, 0)),
                  pl.BlockSpec((tm, km), lambda j, i: (i, 0)),
                  pl.BlockSpec((None, kg, tn), lambda j, i: (l, 0, j)),
                  pl.BlockSpec((None, km, tn), lambda j, i: (l, 0, j)),
                  pl.BlockSpec((None, tm, tn), lambda j, i: (gate_a0 + j, i, 0)),
                  pl.BlockSpec((None, tm, tn), lambda j, i: (gate_b0 + j, i, 0))],
        out_specs=pl.BlockSpec((tm, tn), lambda j, i: (i, j)),
        out_shape=jax.ShapeDtypeStruct((m, d), BF16),
        scratch_shapes=[pltpu.VMEM((kg, tn), BF16), pltpu.VMEM((km, tn), BF16)],
        compiler_params=_params("arbitrary", "arbitrary"),
        name="branch_merge",
    )(o_gla, o_moba, w_gla_out, w_moba_out, proj, proj)


def _matmul_kernel(a_ref, w_ref, o_ref, wbf_ref):
    @pl.when(pl.program_id(1) == 0)
    def _():
        wbf_ref[...] = w_ref[...].astype(BF16)

    o_ref[...] = _dot(a_ref[...], wbf_ref[...]).astype(o_ref.dtype)


def _matmul(a, w, l, tm, tn, name):
    m, k = a.shape
    n = w.shape[-1]
    return pl.pallas_call(
        _matmul_kernel,
        grid=(n // tn, m // tm),
        in_specs=[pl.BlockSpec((tm, k), lambda j, i: (i, 0)),
                  pl.BlockSpec((None, k, tn), lambda j, i: (l, 0, j))],
        out_specs=pl.BlockSpec((tm, tn), lambda j, i: (i, j)),
        out_shape=jax.ShapeDtypeStruct((m, n), BF16),
        scratch_shapes=[pltpu.VMEM((k, tn), BF16)],
        compiler_params=_params("arbitrary", "arbitrary"),
        name=name,
    )(a, w)


def _ffn_row_permutation(tile):
    g = tile // V7X_SUBLANES
    pos = np.arange(tile)
    time = (pos % V7X_SUBLANES) * g + pos // V7X_SUBLANES
    perm = np.zeros((tile, tile), np.float32)
    perm[pos, time] = 1.0
    return perm


def _post_mix_kernel(x_ref, y_ref, gpost_ref, mod_ref, gpre_ref, perm_ref, o_ref, h_ref,
                     *, gate_row, shift_row, scale_row):
    y = y_ref[...].astype(F32)
    x = x_ref[...] + mod_ref[gate_row:gate_row + 1, :] * _rms(y, gpost_ref[...])
    o_ref[...] = x
    h = _rms(x, gpre_ref[...])
    h = h * (1.0 + mod_ref[scale_row:scale_row + 1, :]) + mod_ref[shift_row:shift_row + 1, :]
    h_ref[...] = _dot(perm_ref[...], h.astype(BF16)).astype(h_ref.dtype)


def _post_mix(x2, y, gpost3, mod3, gpre3, l, seq):
    m, d = x2.shape
    tm = FFN_TILE
    per_batch = seq // tm
    row_spec = pl.BlockSpec((tm, d), lambda i: (i, 0))
    gain_spec = pl.BlockSpec((None, 1, d), lambda i: (l, 0, 0))
    perm = jnp.asarray(_ffn_row_permutation(tm), BF16)
    return pl.pallas_call(
        functools.partial(_post_mix_kernel, gate_row=2, shift_row=3, scale_row=4),
        grid=(m // tm,),
        in_specs=[row_spec, row_spec, gain_spec,
                  pl.BlockSpec((None, N_MOD, d), lambda i: (i // per_batch, 0, 0)),
                  gain_spec,
                  pl.BlockSpec((tm, tm), lambda i: (0, 0))],
        out_specs=(row_spec, row_spec),
        out_shape=(jax.ShapeDtypeStruct((m, d), F32), jax.ShapeDtypeStruct((m, d), BF16)),
        compiler_params=_params("arbitrary"),
        name="post_mix_pre_ffn",
    )(x2, y, gpost3, mod3, gpre3, perm)


def _post_ffn_kernel(x_ref, y_ref, gpost_ref, mod_ref, unperm_ref, o_ref, *, gate_row):
    y = _dot(unperm_ref[...], y_ref[...])
    o_ref[...] = x_ref[...] + mod_ref[gate_row:gate_row + 1, :] * _rms(y, gpost_ref[...])


def _post_ffn(x2, y, gpost3, mod3, l, seq):
    m, d = x2.shape
    tm = FFN_TILE
    per_batch = seq // tm
    row_spec = pl.BlockSpec((tm, d), lambda i: (i, 0))
    unperm = jnp.asarray(_ffn_row_permutation(tm).T, BF16)
    return pl.pallas_call(
        functools.partial(_post_ffn_kernel, gate_row=5),
        grid=(m // tm,),
        in_specs=[row_spec, row_spec,
                  pl.BlockSpec((None, 1, d), lambda i: (l, 0, 0)),
                  pl.BlockSpec((None, N_MOD, d), lambda i: (i // per_batch, 0, 0)),
                  pl.BlockSpec((tm, tm), lambda i: (0, 0))],
        out_specs=row_spec,
        out_shape=jax.ShapeDtypeStruct((m, d), F32),
        compiler_params=_params("arbitrary"),
        name="post_ffn",
    )(x2, y, gpost3, mod3, unperm)


def _ffn_up_kernel(h_ref, wu_ref, wg_ref, cu_ref, cg_ref, bu_ref, bg_ref, o_ref,
                   wbf_ref, up_even_ref, up_odd_ref, *, row_tiles, tiles_per_batch, n_pairs):
    s = pl.program_id(0)
    tm, tn = o_ref.shape
    head = 2 * V7X_SUBLANES
    g_lane0 = tn + FFN_PAD_LANES
    matmul_starts_batch = (s % row_tiles) % tiles_per_batch == 0

    @pl.when(s == 0)
    def _():
        up_even_ref[...] = jnp.zeros_like(up_even_ref)
        up_odd_ref[...] = jnp.zeros_like(up_odd_ref)

    @pl.when(jnp.logical_and(s % row_tiles == 0, s < n_pairs))
    def _():
        wbf_ref[:, :tn] = wu_ref[...].astype(BF16)
        wbf_ref[:, tn:] = wg_ref[...].astype(BF16)

    def step(cur_ref, prev_ref):
        sub = V7X_SUBLANES

        def conv(r0, half, w_ref, b_ref):
            cols = slice(half * g_lane0, half * g_lane0 + tn)
            acc = b_ref[...]
            for tap in range(CONV_WIDTH):
                start = pl.multiple_of(r0 + tap * sub, sub)
                acc = acc + w_ref[tap:tap + 1, :] * prev_ref[pl.ds(start, FFN_ROW_GROUP), cols]
            return acc

        def chunk(c, carry):
            r0 = pl.multiple_of(c * FFN_ROW_CHUNK, FFN_ROW_CHUNK)
            up = _dot(h_ref[pl.ds(r0, FFN_ROW_CHUNK), :], wbf_ref[...])
            cur_ref[pl.ds(r0 + head, FFN_ROW_CHUNK), 0:tn] = up[:, :tn]
            cur_ref[pl.ds(r0 + head, FFN_ROW_CHUNK), g_lane0:g_lane0 + tn] = up[:, tn:]
            for r in range(0, FFN_ROW_CHUNK, FFN_ROW_GROUP):
                rg = pl.multiple_of(r0 + r, FFN_ROW_GROUP)
                u = conv(rg, 0, cu_ref, bu_ref)
                g = conv(rg, 1, cg_ref, bg_ref)
                gelu = 0.5 * g * (1.0 + jnp.tanh(np.sqrt(2.0 / np.pi) * (g + 0.044715 * (g * g * g))))
                o_ref[pl.ds(rg, FFN_ROW_GROUP), :] = (gelu * u).astype(o_ref.dtype)
            return carry

        lax.fori_loop(0, tm // FFN_ROW_CHUNK, chunk, 0)

        own = pltpu.roll(cur_ref[tm:tm + head, :], 1, axis=0)
        before = jnp.where(matmul_starts_batch, 0.0,
                           pltpu.roll(prev_ref[tm:tm + head, :], 1 + V7X_SUBLANES, axis=0))
        first = lax.broadcasted_iota(jnp.int32, (head, 1), 0) % V7X_SUBLANES == 0
        cur_ref[0:head, :] = jnp.where(first, before, own)

    @pl.when(s % 2 == 0)
    def _():
        step(up_even_ref, up_odd_ref)

    @pl.when(s % 2 == 1)
    def _():
        step(up_odd_ref, up_even_ref)


def _ffn_up(h, w_up, w_conv, b_conv3, l, seq):
    m, d = h.shape
    f = w_up.shape[-1] // 2
    tm, tn = FFN_TILE, 512
    nt = f // tn
    row_tiles = m // tm
    n_pairs = nt * row_tiles
    kern = functools.partial(_ffn_up_kernel, row_tiles=row_tiles, tiles_per_batch=seq // tm,
                             n_pairs=n_pairs)
    mm = lambda s: jnp.minimum(s, n_pairs - 1)
    ep = lambda s: jnp.maximum(s - 1, 0)
    return pl.pallas_call(
        kern,
        grid=(n_pairs + 1,),
        in_specs=[pl.BlockSpec((tm, d), lambda s: (mm(s) % row_tiles, 0)),
                  pl.BlockSpec((None, d, tn), lambda s: (l, 0, mm(s) // row_tiles)),
                  pl.BlockSpec((None, d, tn), lambda s: (l, 0, nt + mm(s) // row_tiles)),
                  pl.BlockSpec((None, CONV_WIDTH, tn), lambda s: (l, 0, ep(s) // row_tiles)),
                  pl.BlockSpec((None, CONV_WIDTH, tn), lambda s: (l, 0, nt + ep(s) // row_tiles)),
                  pl.BlockSpec((None, 1, tn), lambda s: (l, 0, ep(s) // row_tiles)),
                  pl.BlockSpec((None, 1, tn), lambda s: (l, 0, nt + ep(s) // row_tiles))],
        out_specs=pl.BlockSpec((tm, tn), lambda s: (ep(s) % row_tiles, ep(s) // row_tiles)),
        out_shape=jax.ShapeDtypeStruct((m, f), BF16),
        scratch_shapes=[pltpu.VMEM((d, 2 * tn), BF16),
                        pltpu.VMEM((tm + 2 * V7X_SUBLANES, 2 * tn + FFN_PAD_LANES), F32),
                        pltpu.VMEM((tm + 2 * V7X_SUBLANES, 2 * tn + FFN_PAD_LANES), F32)],
        compiler_params=_params("arbitrary"),
        name="ffn_up_conv_gate",
    )(h, w_up, w_up, w_conv, w_conv, b_conv3, b_conv3)


def kernel(x, c, w_ada, b_ada, g_pre_mix, w_in, w_alpha_up, b_alpha, g_gla_head, w_gla_out,
           w_moba_out, w_o, g_post_mix, g_pre_ffn, w_up, w_conv, b_conv, w_down, g_post_ffn):
    batch, seq, d = x.shape
    depth = w_ada.shape[0]
    m = batch * seq
    gla_k = w_alpha_up.shape[-1]
    gla_heads = gla_k // GLA_DK
    gla_dv = g_gla_head.shape[-1]
    gla_v = w_gla_out.shape[1]
    moba_w = w_moba_out.shape[1]
    moba_heads = moba_w // MOBA_HD
    low_col = 2 * gla_k + 2 * gla_v
    assert gla_v == gla_heads * gla_dv and d % IN_TILE == 0
    assert low_col % IN_TILE == 0 and w_in.shape[-1] == low_col + GLA_RANK + 3 * moba_w + 2 * d
    first_shifted = low_col // IN_TILE
    n_tiles = first_shifted + 3 * moba_w // IN_TILE + 2 * d // IN_TILE

    x2 = x.reshape(m, d)
    c8 = jnp.pad(c, ((0, 8 - batch), (0, 0)))
    as3 = lambda a: a.reshape(a.shape[0], 1, a.shape[-1])
    wup_pad = jnp.pad(w_alpha_up, ((0, 0), (0, V7X_LANES - GLA_RANK), (0, 0)))
    w_in_t = jnp.swapaxes(w_in, 1, 2)

    for l in range(depth):
        mod = _modulation(c8, w_ada, as3(b_ada), l)
        mod3 = mod[:batch].reshape(batch, N_MOD, d)
        h, log_a = _prenorm(x2, as3(g_pre_mix), mod3, w_in_t, wup_pad, as3(b_alpha), l, seq,
                            shift_row=0, scale_row=1, low_block=low_col // V7X_LANES)
        proj = _in_projection(h, w_in_t, l, n_tiles, first_shifted)
        o_gla = _gla(proj, log_a, as3(g_gla_head), l, batch, seq, gla_heads, gla_dv)
        o_moba = _moba(proj, batch, seq, moba_heads)
        merged = _merge(o_gla, o_moba, w_gla_out, w_moba_out, proj, l, d,
                        gate_a0=first_shifted + 3 * moba_w // IN_TILE)
        y = _matmul(merged, w_o, l, 1024, 1024, "out_projection")
        x2, h2 = _post_mix(x2, y, as3(g_post_mix), mod3, as3(g_pre_ffn), l, seq)
        act = _ffn_up(h2, w_up, w_conv, as3(b_conv), l, seq)
        y2 = _matmul(act, w_down, l, 512, 512, "ffn_down")
        x2 = _post_ffn(x2, y2, as3(g_post_ffn), mod3, l, seq)
    return x2.reshape(batch, seq, d)
```
